```python
import jax, jax.numpy as jnp
from jax import lax
import numpy as np

D_MODEL = 1024
BATCH = 4
SEQ = 8192
DEPTH = 2

EPS = 1e-6
ROPE_THETA = 10000.0
BLOCK = 128
MLA_HEADS = 4
MLA_Q_RANK = 256
MLA_KV_RANK = 128
MLA_NOPE = 128
MLA_ROPE = 64
MLA_V = 128
MLA_QK = MLA_NOPE + MLA_ROPE
MLA_WIDTH = MLA_HEADS * MLA_V
SWA_HEADS = 8
SWA_KV_HEADS = 2
SWA_HEAD_DIM = 64
WINDOW = 128
SWA_WIDTH = SWA_HEADS * SWA_HEAD_DIM
MIX_WIDTH = MLA_WIDTH + SWA_WIDTH
IN_SPLITS = (MLA_Q_RANK, MLA_KV_RANK, MLA_ROPE,
             SWA_HEADS * SWA_HEAD_DIM, SWA_KV_HEADS * SWA_HEAD_DIM, SWA_KV_HEADS * SWA_HEAD_DIM)
IN_COLS = sum(IN_SPLITS)
D_FF = 2816

kernel_name = "hybrid_mla_swa_sink_macaron"


def rmsnorm(t, g):
    tf = t.astype(jnp.float32)
    out = tf * lax.rsqrt(jnp.mean(tf * tf, axis=-1, keepdims=True) + EPS)
    return (out * g.astype(jnp.float32)).astype(t.dtype)


def swiglu(t, w_gate, w_up, w_down):
    return (jax.nn.silu(t @ w_gate) * (t @ w_up)) @ w_down


def rope_table(seq, dim):
    pos = jnp.arange(seq, dtype=jnp.float32)
    inv = 1.0 / (ROPE_THETA ** (jnp.arange(0, dim, 2, dtype=jnp.float32) / dim))
    ang = pos[:, None] * inv[None, :]
    return jnp.cos(ang), jnp.sin(ang)


def apply_rope(t, cos, sin):
    half = t.shape[-1] // 2
    t1 = t[..., :half].astype(jnp.float32)
    t2 = t[..., half:].astype(jnp.float32)
    c = cos[None, :, None, :]
    s = sin[None, :, None, :]
    return jnp.concatenate([t1 * c - t2 * s, t2 * c + t1 * s], axis=-1).astype(t.dtype)


def dense_causal_attention(q, k, v, scale):
    B, S, H, Dq = q.shape
    nb = S // BLOCK
    qb = q.reshape(B, nb, BLOCK, H, Dq).transpose(1, 0, 2, 3, 4)
    k_pos = jnp.arange(S)

    def one_block(args):
        q_blk, i = args
        s = jnp.einsum('bqhd,bkhd->bhqk', q_blk, k, preferred_element_type=jnp.float32) * scale
        q_pos = i * BLOCK + jnp.arange(BLOCK)
        mask = k_pos[None, :] <= q_pos[:, None]
        s = jnp.where(mask[None, None], s, -jnp.inf)
        p = jax.nn.softmax(s, axis=-1)
        return jnp.einsum('bhqk,bkhd->bqhd', p.astype(v.dtype), v)

    out = lax.map(one_block, (qb, jnp.arange(nb)))
    return out.transpose(1, 0, 2, 3, 4).reshape(B, S, H, v.shape[-1])


def sliding_window_sink_attention(q, k, v, sinks, scale):
    B, S, H, D = q.shape
    KV = k.shape[2]
    G = H // KV
    nb = S // BLOCK
    qb = q.reshape(B, nb, BLOCK, KV, G, D)

    def band(t):
        tb = t.reshape(B, nb, BLOCK, KV, D)
        prev = jnp.pad(tb[:, :-1], ((0, 0), (1, 0), (0, 0), (0, 0), (0, 0)))
        return jnp.concatenate([prev, tb], axis=2)

    kb, vb = band(k), band(v)
    s = jnp.einsum('bnqcgd,bnkcd->bncgqk', qb, kb, preferred_element_type=jnp.float32) * scale
    q_rel = jnp.arange(BLOCK)[:, None] + BLOCK
    k_rel = jnp.arange(2 * BLOCK)[None, :]
    dist = q_rel - k_rel
    in_window = (dist >= 0) & (dist < WINDOW)
    k_abs = jnp.arange(nb)[:, None, None] * BLOCK + k_rel[None] - BLOCK
    valid = in_window[None] & (k_abs >= 0)
    s = jnp.where(valid[None, :, None, None], s, -jnp.inf)
    sink = sinks.astype(jnp.float32).reshape(KV, G)[None, None, :, :, None, None]
    m = jnp.maximum(jnp.max(s, axis=-1, keepdims=True), sink)
    e = jnp.exp(s - m)
    p = e / (jnp.sum(e, axis=-1, keepdims=True) + jnp.exp(sink - m))
    out = jnp.einsum('bncgqk,bnkcd->bnqcgd', p.astype(v.dtype), vb)
    return out.reshape(B, S, H, D)


def setup_inputs(seed: int = 0) -> dict:
    key = jax.random.key(seed)
    ks = iter(jax.random.split(key, 32))

    def w(shape, fan_in):
        return jax.random.normal(next(ks), shape, jnp.float32) * (fan_in ** -0.5)

    def gain(shape):
        return 1.0 + 0.02 * jax.random.normal(next(ks), shape, jnp.float32)

    L = DEPTH
    return {
        "x": jax.random.normal(next(ks), (BATCH, SEQ, D_MODEL), jnp.float32),
        "ffn1_norm": gain((L, D_MODEL)),
        "ffn1_w_gate": w((L, D_MODEL, D_FF), D_MODEL),
        "ffn1_w_up": w((L, D_MODEL, D_FF), D_MODEL),
        "ffn1_w_down": w((L, D_FF, D_MODEL), D_FF),
        "mix_norm": gain((L, D_MODEL)),
        "w_in": w((L, D_MODEL, IN_COLS), D_MODEL),
        "mla_q_a_norm": gain((L, MLA_Q_RANK)),
        "mla_w_q_b": w((L, MLA_Q_RANK, MLA_HEADS * MLA_QK), MLA_Q_RANK),
        "mla_kv_a_norm": gain((L, MLA_KV_RANK)),
        "mla_w_kv_b": w((L, MLA_KV_RANK, MLA_HEADS * (MLA_NOPE + MLA_V)), MLA_KV_RANK),
        "mla_q_norm": gain((L, MLA_QK)),
        "mla_k_norm": gain((L, MLA_QK)),
        "swa_q_norm": gain((L, SWA_HEAD_DIM)),
        "swa_k_norm": gain((L, SWA_HEAD_DIM)),
        "swa_sinks": 0.5 * jax.random.normal(next(ks), (L, SWA_HEADS), jnp.float32),
        "mla_out_norm": gain((L, MLA_WIDTH)),
        "swa_out_norm": gain((L, SWA_WIDTH)),
        "w_o": w((L, MIX_WIDTH, D_MODEL), MIX_WIDTH),
        "ffn2_norm": gain((L, D_MODEL)),
        "ffn2_w_gate": w((L, D_MODEL, D_FF), D_MODEL),
        "ffn2_w_up": w((L, D_MODEL, D_FF), D_MODEL),
        "ffn2_w_down": w((L, D_FF, D_MODEL), D_FF),
    }


def reference(x, ffn1_norm, ffn1_w_gate, ffn1_w_up, ffn1_w_down, mix_norm, w_in,
              mla_q_a_norm, mla_w_q_b, mla_kv_a_norm, mla_w_kv_b, mla_q_norm, mla_k_norm,
              swa_q_norm, swa_k_norm, swa_sinks, mla_out_norm, swa_out_norm, w_o,
              ffn2_norm, ffn2_w_gate, ffn2_w_up, ffn2_w_down):
    B, S, _ = x.shape
    cos, sin = rope_table(S, MLA_ROPE)
    split_idx = np.cumsum(IN_SPLITS)[:-1].tolist()
    mla_scale = MLA_QK ** -0.5
    swa_scale = SWA_HEAD_DIM ** -0.5

    for l in range(DEPTH):
        x = x + 0.5 * swiglu(rmsnorm(x, ffn1_norm[l]), ffn1_w_gate[l], ffn1_w_up[l], ffn1_w_down[l])

        h = rmsnorm(x, mix_norm[l])
        c_q, c_kv, k_pe, q_s, k_s, v_s = jnp.split(h @ w_in[l], split_idx, axis=-1)

        q_a = (rmsnorm(c_q, mla_q_a_norm[l]) @ mla_w_q_b[l]).reshape(B, S, MLA_HEADS, MLA_QK)
        kv_a = (rmsnorm(c_kv, mla_kv_a_norm[l]) @ mla_w_kv_b[l]).reshape(B, S, MLA_HEADS, MLA_NOPE + MLA_V)
        k_nope, v_a = kv_a[..., :MLA_NOPE], kv_a[..., MLA_NOPE:]
        k_a = jnp.concatenate(
            [k_nope, jnp.broadcast_to(k_pe[:, :, None, :], (B, S, MLA_HEADS, MLA_ROPE))], axis=-1)
        q_a = rmsnorm(q_a, mla_q_norm[l])
        k_a = rmsnorm(k_a, mla_k_norm[l])
        q_a = jnp.concatenate([q_a[..., :MLA_NOPE], apply_rope(q_a[..., MLA_NOPE:], cos, sin)], axis=-1)
        k_a = jnp.concatenate([k_a[..., :MLA_NOPE], apply_rope(k_a[..., MLA_NOPE:], cos, sin)], axis=-1)
        out_a = dense_causal_attention(q_a, k_a, v_a, mla_scale).reshape(B, S, MLA_WIDTH)

        q_b = rmsnorm(q_s.reshape(B, S, SWA_HEADS, SWA_HEAD_DIM), swa_q_norm[l])
        k_b = rmsnorm(k_s.reshape(B, S, SWA_KV_HEADS, SWA_HEAD_DIM), swa_k_norm[l])
        v_b = v_s.reshape(B, S, SWA_KV_HEADS, SWA_HEAD_DIM)
        q_b = apply_rope(q_b, cos, sin)
        k_b = apply_rope(k_b, cos, sin)
        out_b = sliding_window_sink_attention(q_b, k_b, v_b, swa_sinks[l], swa_scale).reshape(B, S, SWA_WIDTH)

        mixed = jnp.concatenate([rmsnorm(out_a, mla_out_norm[l]), rmsnorm(out_b, swa_out_norm[l])], axis=-1)
        x = x + mixed @ w_o[l]

        x = x + 0.5 * swiglu(rmsnorm(x, ffn2_norm[l]), ffn2_w_gate[l], ffn2_w_up[l], ffn2_w_down[l])
    return x
```

```python
import functools

import numpy as np
import jax
import jax.numpy as jnp
from jax import lax
from jax.experimental import pallas as pl
from jax.experimental.pallas import tpu as pltpu

D_MODEL = 1024
EPS = 1e-6
ROPE_THETA = 10000.0
MLA_HEADS = 4
MLA_Q_RANK = 256
MLA_KV_RANK = 128
MLA_NOPE = 128
MLA_ROPE = 64
MLA_V = 128
MLA_QK = MLA_NOPE + MLA_ROPE
MLA_WIDTH = MLA_HEADS * MLA_V
SWA_HEADS = 8
SWA_KV_HEADS = 2
SWA_GROUP = SWA_HEADS // SWA_KV_HEADS
SWA_HEAD_DIM = 64
SWA_BLOCK = 128
SWA_WIDTH = SWA_HEADS * SWA_HEAD_DIM
D_FF = 2816

LANES = 128
HALF = SWA_HEAD_DIM // 2
MLA_QK_PAD = 2 * LANES
NEG_BIG = -1e30

VMEM_LIMIT = 56 * 1024 * 1024

TM_ROWS = 512
MLA_TQ = 512
SWA_TQ = 512

BF16 = jnp.bfloat16
F32 = jnp.float32


def _rms(t, gain, width):
    ss = jnp.sum(t * t, axis=-1, keepdims=True)
    return t * lax.rsqrt(ss * (1.0 / width) + EPS) * gain


def _dot(a, b):
    return jnp.dot(a, b, preferred_element_type=F32)


def _dot_nt(a, b):
    return lax.dot_general(a, b, (((1,), (1,)), ((), ())), preferred_element_type=F32)


def _ffn_block(x, g, wg_ref, wu_ref, wd_ref):
    h = _rms(x, g, D_MODEL).astype(BF16)
    gate = _dot(h, wg_ref[...])
    up = _dot(h, wu_ref[...])
    act = (gate * jax.nn.sigmoid(gate) * up).astype(BF16)
    return x + 0.5 * _dot(act, wd_ref[...])


def _ffn_kernel(x_ref, g_ref, wg_ref, wu_ref, wd_ref, o_ref):
    o_ref[...] = _ffn_block(x_ref[...], g_ref[...], wg_ref, wu_ref, wd_ref)


def _const_spec(shape):
    nd = len(shape)
    return pl.BlockSpec(shape, lambda *_: (0,) * nd, pipeline_mode=pl.Buffered(1))


def _row_spec(tm, width):
    return pl.BlockSpec((tm, width), lambda i: (i, 0))


def _ffn_call(x, g, wg, wu, wd):
    n = x.shape[0]
    tm = min(TM_ROWS, n)
    return pl.pallas_call(
        _ffn_kernel,
        grid=(n // tm,),
        in_specs=[_row_spec(tm, D_MODEL), _const_spec(g.shape), _const_spec(wg.shape),
                  _const_spec(wu.shape), _const_spec(wd.shape)],
        out_specs=_row_spec(tm, D_MODEL),
        out_shape=jax.ShapeDtypeStruct(x.shape, F32),
        compiler_params=pltpu.CompilerParams(
            dimension_semantics=("arbitrary",), vmem_limit_bytes=VMEM_LIMIT),
        name="ffn",
    )(x, g, wg, wu, wd)


C_CQ = 0
C_CKV = C_CQ + MLA_Q_RANK
C_KPE = C_CKV + MLA_KV_RANK
C_QS = C_KPE + LANES
C_KS = C_QS + SWA_HEADS // 2 * LANES
C_VS = C_KS + SWA_KV_HEADS * LANES
C_END = C_VS + SWA_KV_HEADS * SWA_HEAD_DIM


def _rope(t, cos, sin_signed):
    return t * cos + pltpu.roll(t, 2 * HALF, 1) * sin_signed


def _proj_kernel(x_ref, g_ref, win_ref, gqa_ref, wqb_ref, gkva_ref, wkvb_ref,
                 gqn_ref, gqr_ref, gkn_ref, gkr_ref, gsq_ref, gsk_ref, cos_ref, sin_ref,
                 qa_ref, ka_ref, va_ref, qb_ref, kb_ref, vb_ref):
    mla_scale = MLA_QK ** -0.5
    swa_scale = SWA_HEAD_DIM ** -0.5
    cos = cos_ref[...]
    sin = sin_ref[...]
    lane = lax.broadcasted_iota(jnp.int32, (1, LANES), 1)
    first = (lane & (2 * HALF - 1)) < HALF
    mask = (first.astype(F32), 1.0 - first.astype(F32))

    h = _rms(x_ref[...], g_ref[...], D_MODEL).astype(BF16)
    proj = _dot(h, win_ref[...])

    cq = _rms(proj[:, C_CQ:C_CKV], gqa_ref[...], MLA_Q_RANK).astype(BF16)
    q_up = _dot(cq, wqb_ref[...])
    ckv = _rms(proj[:, C_CKV:C_KPE], gkva_ref[...], MLA_KV_RANK).astype(BF16)
    kv_up = _dot(ckv, wkvb_ref[...])

    rope_base = MLA_HEADS * MLA_NOPE
    for pair in range(MLA_HEADS // 2):
        xr = q_up[:, rope_base + pair * LANES: rope_base + (pair + 1) * LANES]
        xr2 = xr * xr
        inv = []
        for sub in range(2):
            hd = 2 * pair + sub
            nope = q_up[:, hd * MLA_NOPE:(hd + 1) * MLA_NOPE]
            ss = (jnp.sum(nope * nope, axis=-1, keepdims=True)
                  + jnp.sum(xr2 * mask[sub], axis=-1, keepdims=True))
            r = lax.rsqrt(ss * (1.0 / MLA_QK) + EPS)
            inv.append(r)
            qa_ref[:, hd * MLA_QK_PAD: hd * MLA_QK_PAD + MLA_NOPE] = (
                nope * (r * mla_scale) * gqn_ref[...]).astype(BF16)
        xs = xr * jnp.where(first, inv[0], inv[1]) * gqr_ref[...]
        xrot = _rope(xs, cos, sin) * mla_scale
        for sub in range(2):
            hd = 2 * pair + sub
            qa_ref[:, hd * MLA_QK_PAD + MLA_NOPE:(hd + 1) * MLA_QK_PAD] = (
                xrot * mask[sub]).astype(BF16)

    kpe = proj[:, C_KPE:C_QS]
    ss_pe = 0.5 * jnp.sum(kpe * kpe, axis=-1, keepdims=True)
    kpe_rot = _rope(kpe * gkr_ref[...], cos, sin)
    for hd in range(MLA_HEADS):
        nope = kv_up[:, hd * MLA_NOPE:(hd + 1) * MLA_NOPE]
        ss = jnp.sum(nope * nope, axis=-1, keepdims=True) + ss_pe
        r = lax.rsqrt(ss * (1.0 / MLA_QK) + EPS)
        ka_ref[:, hd * MLA_QK_PAD: hd * MLA_QK_PAD + MLA_NOPE] = (
            nope * r * gkn_ref[...]).astype(BF16)
        ka_ref[:, hd * MLA_QK_PAD + MLA_NOPE:(hd + 1) * MLA_QK_PAD] = (kpe_rot * r).astype(BF16)
    va_ref[...] = kv_up[:, MLA_HEADS * MLA_NOPE:].astype(BF16)

    for pair in range(SWA_HEADS // 2):
        xq = proj[:, C_QS + pair * LANES: C_QS + (pair + 1) * LANES]
        xq2 = xq * xq
        inv = [lax.rsqrt(jnp.sum(xq2 * mask[sub], axis=-1, keepdims=True) * (1.0 / SWA_HEAD_DIM) + EPS)
               for sub in range(2)]
        xs = xq * jnp.where(first, inv[0], inv[1]) * gsq_ref[...]
        xrot = _rope(xs, cos, sin) * swa_scale
        for sub in range(2):
            hd = 2 * pair + sub
            qb_ref[:, hd * LANES:(hd + 1) * LANES] = (xrot * mask[sub]).astype(BF16)
    for c in range(SWA_KV_HEADS):
        xk = proj[:, C_KS + c * LANES: C_KS + (c + 1) * LANES]
        ss = 0.5 * jnp.sum(xk * xk, axis=-1, keepdims=True)
        r = lax.rsqrt(ss * (1.0 / SWA_HEAD_DIM) + EPS)
        kb_ref[:, c * LANES:(c + 1) * LANES] = _rope(xk * r * gsk_ref[...], cos, sin).astype(BF16)
    vb_ref[...] = proj[:, C_VS:C_END].astype(BF16)


def _proj_call(x, seq, consts, cos, sin):
    n = x.shape[0]
    tm = min(TM_ROWS, seq)
    n_seq_tiles = seq // tm
    pos_spec = pl.BlockSpec((tm, LANES), lambda i: (i % n_seq_tiles, 0))
    widths = (MLA_HEADS * MLA_QK_PAD, MLA_HEADS * MLA_QK_PAD, MLA_WIDTH,
              SWA_HEADS * LANES, SWA_KV_HEADS * LANES, SWA_KV_HEADS * SWA_HEAD_DIM)
    return pl.pallas_call(
        _proj_kernel,
        grid=(n // tm,),
        in_specs=[_row_spec(tm, D_MODEL)] + [_const_spec(c.shape) for c in consts] + [pos_spec, pos_spec],
        out_specs=[_row_spec(tm, w) for w in widths],
        out_shape=[jax.ShapeDtypeStruct((n, w), BF16) for w in widths],
        compiler_params=pltpu.CompilerParams(
            dimension_semantics=("arbitrary",), vmem_limit_bytes=VMEM_LIMIT),
        name="proj",
    )(x, *consts, cos, sin)


def _mla_kernel(q_ref, k_ref, v_ref, o_ref):
    t = q_ref.shape[1]
    qi = pl.program_id(2)
    q = q_ref[0]

    def step(j, carry, masked):
        m, l, acc = carry
        start = pl.multiple_of(j * t, t)
        k = k_ref[0, pl.ds(start, t), :]
        v = v_ref[0, pl.ds(start, t), :]
        s = _dot_nt(q, k)
        if masked:
            row = lax.broadcasted_iota(jnp.int32, (t, t), 0)
            col = lax.broadcasted_iota(jnp.int32, (t, t), 1)
            s = jnp.where(col <= row, s, NEG_BIG)
        m_new = jnp.maximum(m, jnp.max(s, axis=-1, keepdims=True))
        p = jnp.exp(s - m_new)
        alpha = jnp.exp(m - m_new)
        l = alpha * l + jnp.sum(p, axis=-1, keepdims=True)
        acc = alpha * acc + _dot(p.astype(BF16), v)
        return m_new, l, acc

    init = (jnp.full((t, 1), NEG_BIG, F32), jnp.zeros((t, 1), F32), jnp.zeros((t, MLA_V), F32))
    carry = lax.fori_loop(0, qi, functools.partial(step, masked=False), init)
    m, l, acc = step(qi, carry, masked=True)
    o_ref[0] = acc / l


def _mla_call(qa, ka, va):
    b, s, _ = qa.shape
    t = min(MLA_TQ, s)
    return pl.pallas_call(
        _mla_kernel,
        grid=(b, MLA_HEADS, s // t),
        in_specs=[pl.BlockSpec((1, t, MLA_QK_PAD), lambda bi, h, i: (bi, i, h)),
                  pl.BlockSpec((1, s, MLA_QK_PAD), lambda bi, h, i: (bi, 0, h)),
                  pl.BlockSpec((1, s, MLA_V), lambda bi, h, i: (bi, 0, h))],
        out_specs=pl.BlockSpec((1, t, MLA_V), lambda bi, h, i: (bi, i, h)),
        out_shape=jax.ShapeDtypeStruct((b, s, MLA_WIDTH), F32),
        compiler_params=pltpu.CompilerParams(
            dimension_semantics=("arbitrary", "arbitrary", "arbitrary"),
            vmem_limit_bytes=VMEM_LIMIT),
        name="mla_attn",
    )(qa, ka, va)


def _swa_kernel(sink_ref, q_ref, kc_ref, kp_ref, vc_ref, vp_ref, o_ref, kf_ref, vf_ref):
    blk = SWA_BLOCK
    n_blk = q_ref.shape[1] // blk
    tile = pl.program_id(1)
    kf_ref[0:blk, :] = kp_ref[0]
    kf_ref[blk:, :] = kc_ref[0]
    vf_ref[0:blk, :] = vp_ref[0]
    vf_ref[blk:, :] = vc_ref[0]

    rows = SWA_GROUP * blk
    q_rel = lax.broadcasted_iota(jnp.int32, (rows, 2 * blk), 0) & (blk - 1)
    k_rel = lax.broadcasted_iota(jnp.int32, (rows, 2 * blk), 1)
    in_window = (k_rel > q_rel) & (k_rel <= q_rel + blk)
    head_of_row = lax.broadcasted_iota(jnp.int32, (rows, 1), 0) // blk
    lane = lax.broadcasted_iota(jnp.int32, (1, LANES), 1)
    low = lane < SWA_HEAD_DIM

    for c in range(SWA_KV_HEADS):
        sink = jnp.zeros((rows, 1), F32)
        for g in range(SWA_GROUP):
            sink = jnp.where(head_of_row == g, sink_ref[c * SWA_GROUP + g], sink)
        for n in range(n_blk):
            q = jnp.concatenate(
                [q_ref[0, n * blk:(n + 1) * blk, (c * SWA_GROUP + g) * LANES:(c * SWA_GROUP + g + 1) * LANES]
                 for g in range(SWA_GROUP)], axis=0)
            k = kf_ref[n * blk:(n + 2) * blk, c * LANES:(c + 1) * LANES]
            v = vf_ref[n * blk:(n + 2) * blk, :]
            s = _dot_nt(q, k)
            valid = in_window
            if n == 0:
                valid = valid & ((k_rel >= blk) | (tile > 0))
            s = jnp.where(valid, s, NEG_BIG)
            m = jnp.maximum(jnp.max(s, axis=-1, keepdims=True), sink)
            e = jnp.exp(s - m)
            denom = jnp.sum(e, axis=-1, keepdims=True) + jnp.exp(sink - m)
            o = _dot(e.astype(BF16), v) / denom
            for j in range(SWA_GROUP // 2):
                a = o[(2 * j) * blk:(2 * j + 1) * blk]
                bb = o[(2 * j + 1) * blk:(2 * j + 2) * blk]
                if c == 0:
                    bb = pltpu.roll(bb, SWA_HEAD_DIM, 1)
                else:
                    a = pltpu.roll(a, SWA_HEAD_DIM, 1)
                col = (c * SWA_GROUP // 2 + j) * LANES
                o_ref[0, n * blk:(n + 1) * blk, col:col + LANES] = jnp.where(low, a, bb)


def _swa_call(qb, kb, vb, sinks):
    b, s, _ = qb.shape
    tq = min(SWA_TQ, s)
    ratio = tq // SWA_BLOCK
    kw = SWA_KV_HEADS * LANES
    vw = SWA_KV_HEADS * SWA_HEAD_DIM
    prev = lambda bi, i: (bi, jnp.maximum(i * ratio - 1, 0), 0)
    cur = lambda bi, i: (bi, i, 0)
    return pl.pallas_call(
        _swa_kernel,
        grid=(b, s // tq),
        in_specs=[pl.BlockSpec(memory_space=pltpu.SMEM),
                  pl.BlockSpec((1, tq, SWA_HEADS * LANES), cur),
                  pl.BlockSpec((1, tq, kw), cur),
                  pl.BlockSpec((1, SWA_BLOCK, kw), prev),
                  pl.BlockSpec((1, tq, vw), cur),
                  pl.BlockSpec((1, SWA_BLOCK, vw), prev)],
        out_specs=pl.BlockSpec((1, tq, SWA_WIDTH), cur),
        out_shape=jax.ShapeDtypeStruct((b, s, SWA_WIDTH), F32),
        scratch_shapes=[pltpu.VMEM((tq + SWA_BLOCK, kw), BF16),
                        pltpu.VMEM((tq + SWA_BLOCK, vw), BF16)],
        compiler_params=pltpu.CompilerParams(
            dimension_semantics=("arbitrary", "arbitrary"), vmem_limit_bytes=VMEM_LIMIT),
        name="swa_attn",
    )(sinks, qb, kb, kb, vb, vb)


def _out_kernel(x_ref, oa_ref, ob_ref, ga_ref, gb_ref, wo_ref, o_ref):
    na = _rms(oa_ref[...], ga_ref[...], MLA_WIDTH).astype(BF16)
    nb = _rms(ob_ref[...], gb_ref[...], SWA_WIDTH).astype(BF16)
    y = _dot(na, wo_ref[0:MLA_WIDTH, :]) + _dot(nb, wo_ref[MLA_WIDTH:, :])
    o_ref[...] = x_ref[...] + y


def _out_call(x, oa, ob, ga, gb, wo):
    n = x.shape[0]
    tm = min(TM_ROWS, n)
    return pl.pallas_call(
        _out_kernel,
        grid=(n // tm,),
        in_specs=[_row_spec(tm, D_MODEL), _row_spec(tm, MLA_WIDTH), _row_spec(tm, SWA_WIDTH),
                  _const_spec(ga.shape), _const_spec(gb.shape), _const_spec(wo.shape)],
        out_specs=_row_spec(tm, D_MODEL),
        out_shape=jax.ShapeDtypeStruct(x.shape, F32),
        compiler_params=pltpu.CompilerParams(
            dimension_semantics=("arbitrary",), vmem_limit_bytes=VMEM_LIMIT),
        name="out_proj",
    )(x, oa, ob, ga, gb, wo)


def _pair_cols(n_heads, head_dim, base=0):
    half = head_dim // 2
    cols = []
    for pair in range(n_heads // 2):
        for part in range(2):
            for sub in range(2):
                start = base + (2 * pair + sub) * head_dim + part * half
                cols.extend(range(start, start + half))
    return np.asarray(cols, np.int32)


def _dup_cols(n_heads, head_dim, base=0):
    half = head_dim // 2
    cols = []
    for hd in range(n_heads):
        for part in range(2):
            start = base + hd * head_dim + part * half
            cols.extend(list(range(start, start + half)) * 2)
    return np.asarray(cols, np.int32)


def _layer_consts(l, mix_norm, w_in, mla_q_a_norm, mla_w_q_b, mla_kv_a_norm, mla_w_kv_b,
                  mla_q_norm, mla_k_norm, swa_q_norm, swa_k_norm):
    o_kpe = MLA_Q_RANK + MLA_KV_RANK
    o_qs = o_kpe + MLA_ROPE
    o_ks = o_qs + SWA_WIDTH
    o_vs = o_ks + SWA_KV_HEADS * SWA_HEAD_DIM
    in_cols = np.concatenate([
        np.arange(0, o_kpe, dtype=np.int32),
        _dup_cols(1, MLA_ROPE, o_kpe),
        _pair_cols(SWA_HEADS, SWA_HEAD_DIM, o_qs),
        _dup_cols(SWA_KV_HEADS, SWA_HEAD_DIM, o_ks),
        np.arange(o_vs, o_vs + SWA_KV_HEADS * SWA_HEAD_DIM, dtype=np.int32)])
    qb_nope = np.concatenate([np.arange(h * MLA_QK, h * MLA_QK + MLA_NOPE) for h in range(MLA_HEADS)])
    qb_rope = np.concatenate([
        np.arange((2 * p + sub) * MLA_QK + MLA_NOPE + part * HALF,
                  (2 * p + sub) * MLA_QK + MLA_NOPE + (part + 1) * HALF)
        for p in range(MLA_HEADS // 2) for part in range(2) for sub in range(2)])
    kv_cols = np.concatenate(
        [np.arange(h * (MLA_NOPE + MLA_V), h * (MLA_NOPE + MLA_V) + MLA_NOPE) for h in range(MLA_HEADS)]
        + [np.arange(h * (MLA_NOPE + MLA_V) + MLA_NOPE, (h + 1) * (MLA_NOPE + MLA_V)) for h in range(MLA_HEADS)])
    rope_gain = _dup_cols(1, MLA_ROPE, MLA_NOPE)
    swa_gain = _dup_cols(1, SWA_HEAD_DIM)
    row = lambda v: v.reshape(1, -1).astype(F32)
    return (
        row(mix_norm[l]),
        w_in[l][:, in_cols].astype(BF16),
        row(mla_q_a_norm[l]),
        mla_w_q_b[l][:, np.concatenate([qb_nope, qb_rope])].astype(BF16),
        row(mla_kv_a_norm[l]),
        mla_w_kv_b[l][:, kv_cols].astype(BF16),
        row(mla_q_norm[l][:MLA_NOPE]),
        row(mla_q_norm[l][rope_gain]),
        row(mla_k_norm[l][:MLA_NOPE]),
        row(mla_k_norm[l][rope_gain]),
        row(swa_q_norm[l][swa_gain]),
        row(swa_k_norm[l][swa_gain]),
    )


def _rope_tables(seq):
    pos = jnp.arange(seq, dtype=F32)
    inv = 1.0 / (ROPE_THETA ** (jnp.arange(0, 2 * HALF, 2, dtype=F32) / (2 * HALF)))
    ang = pos[:, None] * inv[None, :]
    c, s = jnp.cos(ang), jnp.sin(ang)
    return jnp.concatenate([c, c, c, c], axis=1), jnp.concatenate([-s, -s, s, s], axis=1)


def kernel(x, ffn1_norm, ffn1_w_gate, ffn1_w_up, ffn1_w_down, mix_norm, w_in, mla_q_a_norm, mla_w_q_b, mla_kv_a_norm, mla_w_kv_b, mla_q_norm, mla_k_norm, swa_q_norm, swa_k_norm, swa_sinks, mla_out_norm, swa_out_norm, w_o, ffn2_norm, ffn2_w_gate, ffn2_w_up, ffn2_w_down):
    b, s, d = x.shape
    depth = w_in.shape[0]
    cos, sin = _rope_tables(s)
    row = lambda v: v.reshape(1, -1).astype(F32)
    xf = x.reshape(b * s, d)
    for l in range(depth):
        xf = _ffn_call(xf, row(ffn1_norm[l]), ffn1_w_gate[l].astype(BF16),
                       ffn1_w_up[l].astype(BF16), ffn1_w_down[l].astype(BF16))
        consts = _layer_consts(l, mix_norm, w_in, mla_q_a_norm, mla_w_q_b, mla_kv_a_norm,
                               mla_w_kv_b, mla_q_norm, mla_k_norm, swa_q_norm, swa_k_norm)
        qa, ka, va, qb, kb, vb = _proj_call(xf, s, consts, cos, sin)
        shp = lambda t: t.reshape(b, s, t.shape[-1])
        oa = _mla_call(shp(qa), shp(ka), shp(va))
        ob = _swa_call(shp(qb), shp(kb), shp(vb), swa_sinks[l].astype(F32))
        xf = _out_call(xf, oa.reshape(b * s, MLA_WIDTH), ob.reshape(b * s, SWA_WIDTH),
                       row(mla_out_norm[l]), row(swa_out_norm[l]), w_o[l].astype(BF16))
        xf = _ffn_call(xf, row(ffn2_norm[l]), ffn2_w_gate[l].astype(BF16),
                       ffn2_w_up[l].astype(BF16), ffn2_w_down[l].astype(BF16))
    return xf.reshape(b, s, d)
```

```python
import functools

import numpy as np
import jax
import jax.numpy as jnp
from jax import lax
from jax.experimental import pallas as pl
from jax.experimental.pallas import tpu as pltpu

D_MODEL = 1024
EPS = 1e-6
ROPE_THETA = 10000.0
MLA_HEADS = 4
MLA_Q_RANK = 256
MLA_KV_RANK = 128
MLA_NOPE = 128
MLA_ROPE = 64
MLA_V = 128
MLA_QK = MLA_NOPE + MLA_ROPE
MLA_WIDTH = MLA_HEADS * MLA_V
SWA_HEADS = 8
SWA_KV_HEADS = 2
SWA_GROUP = SWA_HEADS // SWA_KV_HEADS
SWA_HEAD_DIM = 64
SWA_BLOCK = 128
SWA_WIDTH = SWA_HEADS * SWA_HEAD_DIM
D_FF = 2816

LANES = 128
HALF = SWA_HEAD_DIM // 2
MLA_QK_PAD = 2 * LANES
NEG_BIG = -1e30

VMEM_LIMIT = 56 * 1024 * 1024

LOG2E = 1.4426950408889634

TM_ROWS = 512
SWA_TQ = 512

BF16 = jnp.bfloat16
F32 = jnp.float32


def _rms(t, gain, width):
    ss = jnp.sum(t * t, axis=-1, keepdims=True)
    return t * lax.rsqrt(ss * (1.0 / width) + EPS) * gain


def _dot(a, b):
    return jnp.dot(a, b, preferred_element_type=F32)


def _dot_nt(a, b):
    return lax.dot_general(a, b, (((1,), (1,)), ((), ())), preferred_element_type=F32)


def _ffn_block(x, g, wg_ref, wu_ref, wd_ref):
    h = _rms(x, g, D_MODEL).astype(BF16)
    gate = _dot(h, wg_ref[...])
    up = _dot(h, wu_ref[...])
    act = (gate * jax.nn.sigmoid(gate) * up).astype(BF16)
    return x + 0.5 * _dot(act, wd_ref[...])


def _ffn_kernel(x_ref, g_ref, wg_ref, wu_ref, wd_ref, o_ref):
    o_ref[...] = _ffn_block(x_ref[...], g_ref[...], wg_ref, wu_ref, wd_ref)


def _const_spec(shape):
    nd = len(shape)
    return pl.BlockSpec(shape, lambda *_: (0,) * nd, pipeline_mode=pl.Buffered(1))


def _row_spec(tm, width):
    return pl.BlockSpec((tm, width), lambda i: (i, 0))


def _ffn_call(x, g, wg, wu, wd):
    n = x.shape[0]
    tm = min(TM_ROWS, n)
    return pl.pallas_call(
        _ffn_kernel,
        grid=(n // tm,),
        in_specs=[_row_spec(tm, D_MODEL), _const_spec(g.shape), _const_spec(wg.shape),
                  _const_spec(wu.shape), _const_spec(wd.shape)],
        out_specs=_row_spec(tm, D_MODEL),
        out_shape=jax.ShapeDtypeStruct(x.shape, F32),
        compiler_params=pltpu.CompilerParams(
            dimension_semantics=("arbitrary",), vmem_limit_bytes=VMEM_LIMIT),
        name="ffn",
    )(x, g, wg, wu, wd)


C_CQ = 0
C_CKV = C_CQ + MLA_Q_RANK
C_KPE = C_CKV + MLA_KV_RANK
C_QS = C_KPE + LANES
C_KS = C_QS + SWA_HEADS // 2 * LANES
C_VS = C_KS + SWA_KV_HEADS * LANES
C_END = C_VS + SWA_KV_HEADS * SWA_HEAD_DIM


def _rope(t, cos, sin_signed):
    return t * cos + pltpu.roll(t, 2 * HALF, 1) * sin_signed


def _proj_kernel(x_ref, g_ref, win_ref, gqa_ref, wqbt_ref, gkva_ref, wkb_ref, wvt_ref,
                 gq_ref, gkn_ref, gkr_ref, gsq_ref, gsk_ref, cos_ref, sin_ref, cost_ref, sint_ref,
                 qt_ref, ka_ref, vt_ref, qb_ref, kb_ref, vb_ref):
    tm = x_ref.shape[0]
    q_scale = MLA_QK ** -0.5 * LOG2E
    swa_scale = SWA_HEAD_DIM ** -0.5
    cos = cos_ref[...]
    sin = sin_ref[...]
    lane = lax.broadcasted_iota(jnp.int32, (1, LANES), 1)
    first = (lane & (2 * HALF - 1)) < HALF
    mask = (first.astype(F32), 1.0 - first.astype(F32))

    h = _rms(x_ref[...], g_ref[...], D_MODEL).astype(BF16)
    proj = _dot(h, win_ref[...])

    cq_t = _rms(proj[:, C_CQ:C_CKV], gqa_ref[...], MLA_Q_RANK).T.astype(BF16)
    q_up = _dot(wqbt_ref[...], cq_t)
    gq = pltpu.repeat(gq_ref[...], tm // LANES, axis=1)
    cos_t = cost_ref[...]
    sin_t = sint_ref[...]
    zeros = jnp.zeros((HALF, tm), BF16)
    for hd in range(MLA_HEADS):
        blk = q_up[hd * MLA_QK:(hd + 1) * MLA_QK]
        ss = jnp.sum(blk * blk, axis=0, keepdims=True)
        qn = blk * (lax.rsqrt(ss * (1.0 / MLA_QK) + EPS) * q_scale) * gq
        t1 = qn[MLA_NOPE:MLA_NOPE + HALF]
        t2 = qn[MLA_NOPE + HALF:]
        base = hd * MLA_QK_PAD
        qt_ref[0, base:base + MLA_NOPE, :] = qn[:MLA_NOPE].astype(BF16)
        qt_ref[0, base + MLA_NOPE:base + MLA_NOPE + HALF, :] = (t1 * cos_t - t2 * sin_t).astype(BF16)
        qt_ref[0, base + MLA_NOPE + HALF:base + MLA_NOPE + 2 * HALF, :] = zeros
        qt_ref[0, base + MLA_NOPE + 2 * HALF:base + MLA_NOPE + 3 * HALF, :] = (
            t2 * cos_t + t1 * sin_t).astype(BF16)
        qt_ref[0, base + MLA_NOPE + 3 * HALF:base + MLA_QK_PAD, :] = zeros

    ckv = _rms(proj[:, C_CKV:C_KPE], gkva_ref[...], MLA_KV_RANK)
    k_up = _dot(ckv.astype(BF16), wkb_ref[...])
    vt_ref[0] = _dot(wvt_ref[...], ckv.T.astype(BF16)).astype(BF16)
    kpe = proj[:, C_KPE:C_QS]
    ss_pe = 0.5 * jnp.sum(kpe * kpe, axis=-1, keepdims=True)
    kpe_rot = _rope(kpe * gkr_ref[...], cos, sin)
    for hd in range(MLA_HEADS):
        nope = k_up[:, hd * MLA_NOPE:(hd + 1) * MLA_NOPE]
        ss = jnp.sum(nope * nope, axis=-1, keepdims=True) + ss_pe
        r = lax.rsqrt(ss * (1.0 / MLA_QK) + EPS)
        ka_ref[:, hd * MLA_QK_PAD: hd * MLA_QK_PAD + MLA_NOPE] = (
            nope * r * gkn_ref[...]).astype(BF16)
        ka_ref[:, hd * MLA_QK_PAD + MLA_NOPE:(hd + 1) * MLA_QK_PAD] = (kpe_rot * r).astype(BF16)

    for pair in range(SWA_HEADS // 2):
        xq = proj[:, C_QS + pair * LANES: C_QS + (pair + 1) * LANES]
        xq2 = xq * xq
        inv = [lax.rsqrt(jnp.sum(xq2 * mask[sub], axis=-1, keepdims=True) * (1.0 / SWA_HEAD_DIM) + EPS)
               for sub in range(2)]
        xs = xq * jnp.where(first, inv[0], inv[1]) * gsq_ref[...]
        xrot = _rope(xs, cos, sin) * swa_scale
        for sub in range(2):
            hd = 2 * pair + sub
            qb_ref[:, hd * LANES:(hd + 1) * LANES] = (xrot * mask[sub]).astype(BF16)
    for c in range(SWA_KV_HEADS):
        xk = proj[:, C_KS + c * LANES: C_KS + (c + 1) * LANES]
        ss = 0.5 * jnp.sum(xk * xk, axis=-1, keepdims=True)
        r = lax.rsqrt(ss * (1.0 / SWA_HEAD_DIM) + EPS)
        kb_ref[:, c * LANES:(c + 1) * LANES] = _rope(xk * r * gsk_ref[...], cos, sin).astype(BF16)
    vb_ref[...] = proj[:, C_VS:C_END].astype(BF16)


def _proj_call(x, seq, consts, tables):
    n = x.shape[0]
    tm = min(TM_ROWS, seq)
    n_seq_tiles = seq // tm
    pos_spec = pl.BlockSpec((tm, LANES), lambda i: (i % n_seq_tiles, 0))
    pos_t_spec = pl.BlockSpec((HALF, tm), lambda i: (0, i % n_seq_tiles))
    row_widths = {1: MLA_HEADS * MLA_QK_PAD, 3: SWA_HEADS * LANES, 4: SWA_KV_HEADS * LANES,
                  5: SWA_KV_HEADS * SWA_HEAD_DIM}
    col_heights = {0: MLA_HEADS * MLA_QK_PAD, 2: MLA_WIDTH}
    out_specs, out_shape = [], []
    for idx in range(6):
        if idx in row_widths:
            out_specs.append(_row_spec(tm, row_widths[idx]))
            out_shape.append(jax.ShapeDtypeStruct((n, row_widths[idx]), BF16))
        else:
            out_specs.append(pl.BlockSpec((1, col_heights[idx], tm), lambda i: (i, 0, 0)))
            out_shape.append(jax.ShapeDtypeStruct((n // tm, col_heights[idx], tm), BF16))
    return pl.pallas_call(
        _proj_kernel,
        grid=(n // tm,),
        in_specs=([_row_spec(tm, D_MODEL)] + [_const_spec(c.shape) for c in consts]
                  + [pos_spec, pos_spec, pos_t_spec, pos_t_spec]),
        out_specs=out_specs,
        out_shape=out_shape,
        compiler_params=pltpu.CompilerParams(
            dimension_semantics=("arbitrary",), vmem_limit_bytes=VMEM_LIMIT),
        name="proj",
    )(x, *consts, *tables)


MLA_STREAMS = 2


def _mla_kernel(q_ref, k_ref, v_ref, o_ref, sa_ref, sb_ref):
    t = q_ref.shape[3]
    tk = t // 2
    qi = pl.program_id(2)
    heads = range(MLA_STREAMS)

    def scores(jj, phase, h):
        start = pl.multiple_of(jj * t, t) + phase * tk
        k = k_ref[0, pl.ds(start, tk), h * MLA_QK_PAD:(h + 1) * MLA_QK_PAD]
        return _dot(k, q_ref[0, 0, h * MLA_QK_PAD:(h + 1) * MLA_QK_PAD, :])

    def consume(jj, phase, h, s, state, masked):
        m, l, acc = state
        if masked:
            key = lax.broadcasted_iota(jnp.int32, (tk, t), 0) + phase * tk
            qry = lax.broadcasted_iota(jnp.int32, (tk, t), 1)
            s = jnp.where(key <= qry, s, NEG_BIG)
        m_new = jnp.maximum(m, jnp.max(s, axis=0, keepdims=True))
        p = jnp.exp2(s - m_new)
        alpha = jnp.exp2(m - m_new)
        l = alpha * l + jnp.sum(p, axis=0, keepdims=True)
        v = v_ref[0, jj, h * MLA_V:(h + 1) * MLA_V, phase * tk:(phase + 1) * tk]
        acc = alpha * acc + _dot(v, p.astype(BF16))
        return m_new, l, acc

    def body(jj, states):
        for h in heads:
            sb_ref[h] = scores(jj, 1, h)
        states = tuple(consume(jj, 0, h, sa_ref[h], states[h], False) for h in heads)
        for h in heads:
            sa_ref[h] = scores(jj + 1, 0, h)
        return tuple(consume(jj, 1, h, sb_ref[h], states[h], False) for h in heads)

    init = tuple((jnp.full((1, t), NEG_BIG, F32), jnp.zeros((1, t), F32), jnp.zeros((MLA_V, t), F32))
                 for _ in heads)
    for h in heads:
        sa_ref[h] = scores(0, 0, h)
    states = lax.fori_loop(0, qi, body, init)
    for h in heads:
        sb_ref[h] = scores(qi, 1, h)
    states = tuple(consume(qi, 0, h, sa_ref[h], states[h], True) for h in heads)
    for h in heads:
        m, l, acc = consume(qi, 1, h, sb_ref[h], states[h], True)
        o_ref[0, :, h * MLA_V:(h + 1) * MLA_V] = (acc / l).T


def _mla_call(qt, ka, vt):
    b, nq, _, t = qt.shape
    s = ka.shape[1]
    qw = MLA_STREAMS * MLA_QK_PAD
    vw = MLA_STREAMS * MLA_V
    return pl.pallas_call(
        _mla_kernel,
        grid=(b, MLA_HEADS // MLA_STREAMS, nq),
        in_specs=[pl.BlockSpec((1, 1, qw, t), lambda bi, h, i: (bi, i, h, 0)),
                  pl.BlockSpec((1, s, qw), lambda bi, h, i: (bi, 0, h)),
                  pl.BlockSpec((1, nq, vw, t), lambda bi, h, i: (bi, 0, h, 0))],
        out_specs=pl.BlockSpec((1, t, vw), lambda bi, h, i: (bi, i, h)),
        out_shape=jax.ShapeDtypeStruct((b, s, MLA_WIDTH), F32),
        scratch_shapes=[pltpu.VMEM((MLA_STREAMS, t // 2, t), F32),
                        pltpu.VMEM((MLA_STREAMS, t // 2, t), F32)],
        compiler_params=pltpu.CompilerParams(
            dimension_semantics=("arbitrary", "arbitrary", "arbitrary"),
            vmem_limit_bytes=VMEM_LIMIT),
        name="mla_attn",
    )(qt, ka, vt)


def _swa_kernel(sink_ref, q_ref, kc_ref, kp_ref, vc_ref, vp_ref, o_ref, kf_ref, vf_ref):
    blk = SWA_BLOCK
    n_blk = q_ref.shape[1] // blk
    tile = pl.program_id(1)
    kf_ref[0:blk, :] = kp_ref[0]
    kf_ref[blk:, :] = kc_ref[0]
    vf_ref[0:blk, :] = vp_ref[0]
    vf_ref[blk:, :] = vc_ref[0]

    rows = SWA_GROUP * blk
    q_rel = lax.broadcasted_iota(jnp.int32, (rows, 2 * blk), 0) & (blk - 1)
    k_rel = lax.broadcasted_iota(jnp.int32, (rows, 2 * blk), 1)
    in_window = (k_rel > q_rel) & (k_rel <= q_rel + blk)
    head_of_row = lax.broadcasted_iota(jnp.int32, (rows, 1), 0) // blk
    lane = lax.broadcasted_iota(jnp.int32, (1, LANES), 1)
    low = lane < SWA_HEAD_DIM

    for c in range(SWA_KV_HEADS):
        sink = jnp.zeros((rows, 1), F32)
        for g in range(SWA_GROUP):
            sink = jnp.where(head_of_row == g, sink_ref[c * SWA_GROUP + g], sink)
        for n in range(n_blk):
            q = jnp.concatenate(
                [q_ref[0, n * blk:(n + 1) * blk, (c * SWA_GROUP + g) * LANES:(c * SWA_GROUP + g + 1) * LANES]
                 for g in range(SWA_GROUP)], axis=0)
            k = kf_ref[n * blk:(n + 2) * blk, c * LANES:(c + 1) * LANES]
            v = vf_ref[n * blk:(n + 2) * blk, :]
            s = _dot_nt(q, k)
            valid = in_window
            if n == 0:
                valid = valid & ((k_rel >= blk) | (tile > 0))
            s = jnp.where(valid, s, NEG_BIG)
            m = jnp.maximum(jnp.max(s, axis=-1, keepdims=True), sink)
            e = jnp.exp(s - m)
            denom = jnp.sum(e, axis=-1, keepdims=True) + jnp.exp(sink - m)
            o = _dot(e.astype(BF16), v) / denom
            for j in range(SWA_GROUP // 2):
                a = o[(2 * j) * blk:(2 * j + 1) * blk]
                bb = o[(2 * j + 1) * blk:(2 * j + 2) * blk]
                if c == 0:
                    bb = pltpu.roll(bb, SWA_HEAD_DIM, 1)
                else:
                    a = pltpu.roll(a, SWA_HEAD_DIM, 1)
                col = (c * SWA_GROUP // 2 + j) * LANES
                o_ref[0, n * blk:(n + 1) * blk, col:col + LANES] = jnp.where(low, a, bb)


def _swa_call(qb, kb, vb, sinks):
    b, s, _ = qb.shape
    tq = min(SWA_TQ, s)
    ratio = tq // SWA_BLOCK
    kw = SWA_KV_HEADS * LANES
    vw = SWA_KV_HEADS * SWA_HEAD_DIM
    prev = lambda bi, i: (bi, jnp.maximum(i * ratio - 1, 0), 0)
    cur = lambda bi, i: (bi, i, 0)
    return pl.pallas_call(
        _swa_kernel,
        grid=(b, s // tq),
        in_specs=[pl.BlockSpec(memory_space=pltpu.SMEM),
                  pl.BlockSpec((1, tq, SWA_HEADS * LANES), cur),
                  pl.BlockSpec((1, tq, kw), cur),
                  pl.BlockSpec((1, SWA_BLOCK, kw), prev),
                  pl.BlockSpec((1, tq, vw), cur),
                  pl.BlockSpec((1, SWA_BLOCK, vw), prev)],
        out_specs=pl.BlockSpec((1, tq, SWA_WIDTH), cur),
        out_shape=jax.ShapeDtypeStruct((b, s, SWA_WIDTH), F32),
        scratch_shapes=[pltpu.VMEM((tq + SWA_BLOCK, kw), BF16),
                        pltpu.VMEM((tq + SWA_BLOCK, vw), BF16)],
        compiler_params=pltpu.CompilerParams(
            dimension_semantics=("arbitrary", "arbitrary"), vmem_limit_bytes=VMEM_LIMIT),
        name="swa_attn",
    )(sinks, qb, kb, kb, vb, vb)


def _out_kernel(x_ref, oa_ref, ob_ref, ga_ref, gb_ref, wo_ref, o_ref):
    na = _rms(oa_ref[...], ga_ref[...], MLA_WIDTH).astype(BF16)
    nb = _rms(ob_ref[...], gb_ref[...], SWA_WIDTH).astype(BF16)
    y = _dot(na, wo_ref[0:MLA_WIDTH, :]) + _dot(nb, wo_ref[MLA_WIDTH:, :])
    o_ref[...] = x_ref[...] + y


def _out_call(x, oa, ob, ga, gb, wo):
    n = x.shape[0]
    tm = min(TM_ROWS, n)
    return pl.pallas_call(
        _out_kernel,
        grid=(n // tm,),
        in_specs=[_row_spec(tm, D_MODEL), _row_spec(tm, MLA_WIDTH), _row_spec(tm, SWA_WIDTH),
                  _const_spec(ga.shape), _const_spec(gb.shape), _const_spec(wo.shape)],
        out_specs=_row_spec(tm, D_MODEL),
        out_shape=jax.ShapeDtypeStruct(x.shape, F32),
        compiler_params=pltpu.CompilerParams(
            dimension_semantics=("arbitrary",), vmem_limit_bytes=VMEM_LIMIT),
        name="out_proj",
    )(x, oa, ob, ga, gb, wo)


def _pair_cols(n_heads, head_dim, base=0):
    half = head_dim // 2
    cols = []
    for pair in range(n_heads // 2):
        for part in range(2):
            for sub in range(2):
                start = base + (2 * pair + sub) * head_dim + part * half
                cols.extend(range(start, start + half))
    return np.asarray(cols, np.int32)


def _dup_cols(n_heads, head_dim, base=0):
    half = head_dim // 2
    cols = []
    for hd in range(n_heads):
        for part in range(2):
            start = base + hd * head_dim + part * half
            cols.extend(list(range(start, start + half)) * 2)
    return np.asarray(cols, np.int32)


def _layer_consts(l, mix_norm, w_in, mla_q_a_norm, mla_w_q_b, mla_kv_a_norm, mla_w_kv_b,
                  mla_q_norm, mla_k_norm, swa_q_norm, swa_k_norm):
    o_kpe = MLA_Q_RANK + MLA_KV_RANK
    o_qs = o_kpe + MLA_ROPE
    o_ks = o_qs + SWA_WIDTH
    o_vs = o_ks + SWA_KV_HEADS * SWA_HEAD_DIM
    in_cols = np.concatenate([
        np.arange(0, o_kpe, dtype=np.int32),
        _dup_cols(1, MLA_ROPE, o_kpe),
        _pair_cols(SWA_HEADS, SWA_HEAD_DIM, o_qs),
        _dup_cols(SWA_KV_HEADS, SWA_HEAD_DIM, o_ks),
        np.arange(o_vs, o_vs + SWA_KV_HEADS * SWA_HEAD_DIM, dtype=np.int32)])
    k_cols = np.concatenate(
        [np.arange(h * (MLA_NOPE + MLA_V), h * (MLA_NOPE + MLA_V) + MLA_NOPE) for h in range(MLA_HEADS)])
    v_cols = np.concatenate(
        [np.arange(h * (MLA_NOPE + MLA_V) + MLA_NOPE, (h + 1) * (MLA_NOPE + MLA_V)) for h in range(MLA_HEADS)])
    rope_gain = _dup_cols(1, MLA_ROPE, MLA_NOPE)
    swa_gain = _dup_cols(1, SWA_HEAD_DIM)
    row = lambda v: v.reshape(1, -1).astype(F32)
    return (
        row(mix_norm[l]),
        w_in[l][:, in_cols].astype(BF16),
        row(mla_q_a_norm[l]),
        mla_w_q_b[l].T.astype(BF16),
        row(mla_kv_a_norm[l]),
        mla_w_kv_b[l][:, k_cols].astype(BF16),
        mla_w_kv_b[l][:, v_cols].T.astype(BF16),
        jnp.broadcast_to(mla_q_norm[l].astype(F32)[:, None], (MLA_QK, LANES)),
        row(mla_k_norm[l][:MLA_NOPE]),
        row(mla_k_norm[l][rope_gain]),
        row(swa_q_norm[l][swa_gain]),
        row(swa_k_norm[l][swa_gain]),
    )


def _rope_tables(seq):
    pos = jnp.arange(seq, dtype=F32)
    inv = 1.0 / (ROPE_THETA ** (jnp.arange(0, 2 * HALF, 2, dtype=F32) / (2 * HALF)))
    ang = pos[:, None] * inv[None, :]
    c, s = jnp.cos(ang), jnp.sin(ang)
    return (jnp.concatenate([c, c, c, c], axis=1), jnp.concatenate([-s, -s, s, s], axis=1), c.T, s.T)


def kernel(x, ffn1_norm, ffn1_w_gate, ffn1_w_up, ffn1_w_down, mix_norm, w_in, mla_q_a_norm, mla_w_q_b, mla_kv_a_norm, mla_w_kv_b, mla_q_norm, mla_k_norm, swa_q_norm, swa_k_norm, swa_sinks, mla_out_norm, swa_out_norm, w_o, ffn2_norm, ffn2_w_gate, ffn2_w_up, ffn2_w_down):
    b, s, d = x.shape
    depth = w_in.shape[0]
    tables = _rope_tables(s)
    row = lambda v: v.reshape(1, -1).astype(F32)
    xf = x.reshape(b * s, d)
    for l in range(depth):
        xf = _ffn_call(xf, row(ffn1_norm[l]), ffn1_w_gate[l].astype(BF16),
                       ffn1_w_up[l].astype(BF16), ffn1_w_down[l].astype(BF16))
        consts = _layer_consts(l, mix_norm, w_in, mla_q_a_norm, mla_w_q_b, mla_kv_a_norm,
                               mla_w_kv_b, mla_q_norm, mla_k_norm, swa_q_norm, swa_k_norm)
        qt, ka, vt, qb, kb, vb = _proj_call(xf, s, consts, tables)
        shp = lambda t: t.reshape(b, s, t.shape[-1])
        tiles = lambda t: t.reshape(b, -1, t.shape[-2], t.shape[-1])
        oa = _mla_call(tiles(qt), shp(ka), tiles(vt))
        ob = _swa_call(shp(qb), shp(kb), shp(vb), swa_sinks[l].astype(F32))
        xf = _out_call(xf, oa.reshape(b * s, MLA_WIDTH), ob.reshape(b * s, SWA_WIDTH),
                       row(mla_out_norm[l]), row(swa_out_norm[l]), w_o[l].astype(BF16))
        xf = _ffn_call(xf, row(ffn2_norm[l]), ffn2_w_gate[l].astype(BF16),
                       ffn2_w_up[l].astype(BF16), ffn2_w_down[l].astype(BF16))
    return xf.reshape(b, s, d)
```

```python
import functools

import numpy as np
import jax
import jax.numpy as jnp
from jax import lax
from jax.experimental import pallas as pl
from jax.experimental.pallas import tpu as pltpu

D_MODEL = 1024
EPS = 1e-6
ROPE_THETA = 10000.0
MLA_HEADS = 4
MLA_Q_RANK = 256
MLA_KV_RANK = 128
MLA_NOPE = 128
MLA_ROPE = 64
MLA_V = 128
MLA_QK = MLA_NOPE + MLA_ROPE
MLA_WIDTH = MLA_HEADS * MLA_V
SWA_HEADS = 8
SWA_KV_HEADS = 2
SWA_GROUP = SWA_HEADS // SWA_KV_HEADS
SWA_HEAD_DIM = 64
SWA_BLOCK = 128
SWA_WIDTH = SWA_HEADS * SWA_HEAD_DIM
D_FF = 2816

LANES = 128
HALF = SWA_HEAD_DIM // 2
MLA_QK_PAD = 2 * LANES
BF16_ROWS = 16
MLA_V_ROWS = MLA_V + BF16_ROWS
NEG_BIG = -1e30

VMEM_LIMIT = 56 * 1024 * 1024

LOG2E = 1.4426950408889634

TM_ROWS = 512
SWA_TQ = 512

BF16 = jnp.bfloat16
F32 = jnp.float32


def _rms(t, gain, width):
    ss = jnp.sum(t * t, axis=-1, keepdims=True)
    return t * lax.rsqrt(ss * (1.0 / width) + EPS) * gain


def _dot(a, b):
    return jnp.dot(a, b, preferred_element_type=F32)


def _dot_nt(a, b):
    return lax.dot_general(a, b, (((1,), (1,)), ((), ())), preferred_element_type=F32)


def _ffn_block(x, g, wg_ref, wu_ref, wd_ref):
    h = _rms(x, g, D_MODEL).astype(BF16)
    gate = _dot(h, wg_ref[...])
    up = _dot(h, wu_ref[...])
    act = (gate * jax.nn.sigmoid(gate) * up).astype(BF16)
    return x + 0.5 * _dot(act, wd_ref[...])


def _ffn_kernel(x_ref, g_ref, wg_ref, wu_ref, wd_ref, o_ref):
    o_ref[...] = _ffn_block(x_ref[...], g_ref[...], wg_ref, wu_ref, wd_ref)


def _const_spec(shape):
    nd = len(shape)
    return pl.BlockSpec(shape, lambda *_: (0,) * nd, pipeline_mode=pl.Buffered(1))


def _row_spec(tm, width):
    return pl.BlockSpec((tm, width), lambda i: (i, 0))


def _ffn_call(x, g, wg, wu, wd):
    n = x.shape[0]
    tm = min(TM_ROWS, n)
    return pl.pallas_call(
        _ffn_kernel,
        grid=(n // tm,),
        in_specs=[_row_spec(tm, D_MODEL), _const_spec(g.shape), _const_spec(wg.shape),
                  _const_spec(wu.shape), _const_spec(wd.shape)],
        out_specs=_row_spec(tm, D_MODEL),
        out_shape=jax.ShapeDtypeStruct(x.shape, F32),
        compiler_params=pltpu.CompilerParams(
            dimension_semantics=("arbitrary",), vmem_limit_bytes=VMEM_LIMIT),
        name="ffn",
    )(x, g, wg, wu, wd)


C_CQ = 0
C_CKV = C_CQ + MLA_Q_RANK
C_KPE = C_CKV + MLA_KV_RANK
C_QS = C_KPE + LANES
C_KS = C_QS + SWA_HEADS // 2 * LANES
C_VS = C_KS + SWA_KV_HEADS * LANES
C_END = C_VS + SWA_KV_HEADS * SWA_HEAD_DIM


def _rope(t, cos, sin_signed):
    return t * cos + pltpu.roll(t, 2 * HALF, 1) * sin_signed


def _proj_kernel(x_ref, g_ref, win_ref, gqa_ref, wqbt_ref, gkva_ref, wkb_ref, wvt_ref,
                 gq_ref, gkn_ref, gkr_ref, gsq_ref, gsk_ref, cos_ref, sin_ref, cost_ref, sint_ref,
                 qt_ref, ka_ref, vt_ref, qb_ref, kb_ref, vb_ref):
    tm = x_ref.shape[0]
    q_scale = MLA_QK ** -0.5 * LOG2E
    swa_scale = SWA_HEAD_DIM ** -0.5
    cos = cos_ref[...]
    sin = sin_ref[...]
    lane = lax.broadcasted_iota(jnp.int32, (1, LANES), 1)
    first = (lane & (2 * HALF - 1)) < HALF
    mask = (first.astype(F32), 1.0 - first.astype(F32))

    h = _rms(x_ref[...], g_ref[...], D_MODEL).astype(BF16)
    proj = _dot(h, win_ref[...])

    cq_t = _rms(proj[:, C_CQ:C_CKV], gqa_ref[...], MLA_Q_RANK).T.astype(BF16)
    q_up = _dot(wqbt_ref[...], cq_t)
    gq = jnp.tile(gq_ref[...], (1, tm // LANES))
    cos_t = cost_ref[...]
    sin_t = sint_ref[...]
    zeros = jnp.zeros((HALF, tm), BF16)
    for hd in range(MLA_HEADS):
        blk = q_up[hd * MLA_QK:(hd + 1) * MLA_QK]
        ss = jnp.sum(blk * blk, axis=0, keepdims=True)
        qn = blk * (lax.rsqrt(ss * (1.0 / MLA_QK) + EPS) * q_scale) * gq
        t1 = qn[MLA_NOPE:MLA_NOPE + HALF]
        t2 = qn[MLA_NOPE + HALF:]
        base = hd * MLA_QK_PAD
        qt_ref[0, base:base + MLA_NOPE, :] = qn[:MLA_NOPE].astype(BF16)
        qt_ref[0, base + MLA_NOPE:base + MLA_NOPE + HALF, :] = (t1 * cos_t - t2 * sin_t).astype(BF16)
        qt_ref[0, base + MLA_NOPE + HALF:base + MLA_NOPE + 2 * HALF, :] = zeros
        qt_ref[0, base + MLA_NOPE + 2 * HALF:base + MLA_NOPE + 3 * HALF, :] = (
            t2 * cos_t + t1 * sin_t).astype(BF16)
        qt_ref[0, base + MLA_NOPE + 3 * HALF:base + MLA_QK_PAD, :] = zeros

    ckv = _rms(proj[:, C_CKV:C_KPE], gkva_ref[...], MLA_KV_RANK)
    k_up = _dot(ckv.astype(BF16), wkb_ref[...])
    v_t = _dot(wvt_ref[...], ckv.T.astype(BF16)).astype(BF16)
    ones_row = (lax.broadcasted_iota(jnp.int32, (BF16_ROWS, tm), 0) == 0).astype(BF16)
    for hd in range(MLA_HEADS):
        vt_ref[0, hd * MLA_V_ROWS:hd * MLA_V_ROWS + MLA_V, :] = v_t[hd * MLA_V:(hd + 1) * MLA_V]
        vt_ref[0, hd * MLA_V_ROWS + MLA_V:(hd + 1) * MLA_V_ROWS, :] = ones_row
    kpe = proj[:, C_KPE:C_QS]
    ss_pe = 0.5 * jnp.sum(kpe * kpe, axis=-1, keepdims=True)
    kpe_rot = _rope(kpe * gkr_ref[...], cos, sin)
    for hd in range(MLA_HEADS):
        nope = k_up[:, hd * MLA_NOPE:(hd + 1) * MLA_NOPE]
        ss = jnp.sum(nope * nope, axis=-1, keepdims=True) + ss_pe
        r = lax.rsqrt(ss * (1.0 / MLA_QK) + EPS)
        ka_ref[:, hd * MLA_QK_PAD: hd * MLA_QK_PAD + MLA_NOPE] = (
            nope * r * gkn_ref[...]).astype(BF16)
        ka_ref[:, hd * MLA_QK_PAD + MLA_NOPE:(hd + 1) * MLA_QK_PAD] = (kpe_rot * r).astype(BF16)

    for pair in range(SWA_HEADS // 2):
        xq = proj[:, C_QS + pair * LANES: C_QS + (pair + 1) * LANES]
        xq2 = xq * xq
        inv = [lax.rsqrt(jnp.sum(xq2 * mask[sub], axis=-1, keepdims=True) * (1.0 / SWA_HEAD_DIM) + EPS)
               for sub in range(2)]
        xs = xq * jnp.where(first, inv[0], inv[1]) * gsq_ref[...]
        xrot = _rope(xs, cos, sin) * swa_scale
        for sub in range(2):
            hd = 2 * pair + sub
            qb_ref[:, hd * LANES:(hd + 1) * LANES] = (xrot * mask[sub]).astype(BF16)
    for c in range(SWA_KV_HEADS):
        xk = proj[:, C_KS + c * LANES: C_KS + (c + 1) * LANES]
        ss = 0.5 * jnp.sum(xk * xk, axis=-1, keepdims=True)
        r = lax.rsqrt(ss * (1.0 / SWA_HEAD_DIM) + EPS)
        kb_ref[:, c * LANES:(c + 1) * LANES] = _rope(xk * r * gsk_ref[...], cos, sin).astype(BF16)
    vb_ref[...] = proj[:, C_VS:C_END].astype(BF16)


def _proj_call(x, seq, consts, tables):
    n = x.shape[0]
    tm = min(TM_ROWS, seq)
    n_seq_tiles = seq // tm
    pos_spec = pl.BlockSpec((tm, LANES), lambda i: (i % n_seq_tiles, 0))
    pos_t_spec = pl.BlockSpec((HALF, tm), lambda i: (0, i % n_seq_tiles))
    row_widths = {1: MLA_HEADS * MLA_QK_PAD, 3: SWA_HEADS * LANES, 4: SWA_KV_HEADS * LANES,
                  5: SWA_KV_HEADS * SWA_HEAD_DIM}
    col_heights = {0: MLA_HEADS * MLA_QK_PAD, 2: MLA_HEADS * MLA_V_ROWS}
    out_specs, out_shape = [], []
    for idx in range(6):
        if idx in row_widths:
            out_specs.append(_row_spec(tm, row_widths[idx]))
            out_shape.append(jax.ShapeDtypeStruct((n, row_widths[idx]), BF16))
        else:
            out_specs.append(pl.BlockSpec((1, col_heights[idx], tm), lambda i: (i, 0, 0)))
            out_shape.append(jax.ShapeDtypeStruct((n // tm, col_heights[idx], tm), BF16))
    return pl.pallas_call(
        _proj_kernel,
        grid=(n // tm,),
        in_specs=([_row_spec(tm, D_MODEL)] + [_const_spec(c.shape) for c in consts]
                  + [pos_spec, pos_spec, pos_t_spec, pos_t_spec]),
        out_specs=out_specs,
        out_shape=out_shape,
        compiler_params=pltpu.CompilerParams(
            dimension_semantics=("arbitrary",), vmem_limit_bytes=VMEM_LIMIT),
        name="proj",
    )(x, *consts, *tables)


MLA_STREAMS = 4


def _mla_kernel(q_ref, k_ref, v_ref, o_ref, s_ref, p_ref, acc_ref):
    t = q_ref.shape[3]
    tk = t // 2
    qi = pl.program_id(2)
    heads = range(MLA_STREAMS)

    def scores(tile, half, h):
        start = pl.multiple_of(tile * t, t) + half * tk
        k = k_ref[0, pl.ds(start, tk), h * MLA_QK_PAD:(h + 1) * MLA_QK_PAD]
        return _dot(k, q_ref[0, 0, h * MLA_QK_PAD:(h + 1) * MLA_QK_PAD, :])

    def weighted_values(tile, half, h, buf):
        v = v_ref[0, tile, h * MLA_V_ROWS:(h + 1) * MLA_V_ROWS, half * tk:(half + 1) * tk]
        return _dot(v, p_ref[buf, h])

    def phase(cur, state, nxt_tile, prev_tile, mask_half=None):
        new_state = []
        for h in heads:
            m, alpha_prev = state[h]
            if prev_tile is not None:
                pv = weighted_values(prev_tile, 1 - cur, h, 1 - cur)
            if nxt_tile is not None:
                s_ref[1 - cur, h] = scores(nxt_tile, 1 - cur, h)
            s = s_ref[cur, h]
            if mask_half is not None:
                key = lax.broadcasted_iota(jnp.int32, (tk, t), 0) + mask_half * tk
                qry = lax.broadcasted_iota(jnp.int32, (tk, t), 1)
                s = jnp.where(key <= qry, s, NEG_BIG)
            m_new = jnp.maximum(m, jnp.max(s, axis=0, keepdims=True))
            p = jnp.exp2(s - m_new)
            p_ref[cur, h] = p.astype(BF16)
            alpha = jnp.exp2(m - m_new)
            if prev_tile is not None:
                acc_ref[h] = alpha_prev * acc_ref[h] + pv
            new_state.append((m_new, alpha))
        return tuple(new_state)

    state = tuple((jnp.full((1, t), NEG_BIG, F32), jnp.zeros((1, t), F32)) for _ in heads)
    for h in heads:
        acc_ref[h] = jnp.zeros((MLA_V_ROWS, t), F32)
        s_ref[0, h] = scores(qi, 0, h)
    state = phase(0, state, qi, None, mask_half=0)
    state = phase(1, state, 0, qi, mask_half=1)

    def body(jj, state):
        prev_tile = jnp.where(jj == 0, qi, jj - 1)
        state = phase(0, state, jj, prev_tile)
        return phase(1, state, jj + 1, jj)

    state = lax.fori_loop(0, qi, body, state)
    last_tile = jnp.maximum(qi - 1, 0)
    for h in heads:
        m, alpha_prev = state[h]
        acc = alpha_prev * acc_ref[h] + weighted_values(last_tile, 1, h, 1)
        o_ref[0, :, h * MLA_V:(h + 1) * MLA_V] = (acc[:MLA_V] / acc[MLA_V:MLA_V + 1]).T


def _mla_call(qt, ka, vt):
    b, nq, _, t = qt.shape
    s = ka.shape[1]
    qw = MLA_STREAMS * MLA_QK_PAD
    vw = MLA_STREAMS * MLA_V_ROWS
    ow = MLA_STREAMS * MLA_V
    return pl.pallas_call(
        _mla_kernel,
        grid=(b, MLA_HEADS // MLA_STREAMS, nq),
        in_specs=[pl.BlockSpec((1, 1, qw, t), lambda bi, h, i: (bi, i, h, 0)),
                  pl.BlockSpec((1, s, qw), lambda bi, h, i: (bi, 0, h), pipeline_mode=pl.Buffered(1)),
                  pl.BlockSpec((1, nq, vw, t), lambda bi, h, i: (bi, 0, h, 0),
                               pipeline_mode=pl.Buffered(1))],
        out_specs=pl.BlockSpec((1, t, ow), lambda bi, h, i: (bi, i, h)),
        out_shape=jax.ShapeDtypeStruct((b, s, MLA_WIDTH), F32),
        scratch_shapes=[pltpu.VMEM((2, MLA_STREAMS, t // 2, t), F32),
                        pltpu.VMEM((2, MLA_STREAMS, t // 2, t), BF16),
                        pltpu.VMEM((MLA_STREAMS, MLA_V_ROWS, t), F32)],
        compiler_params=pltpu.CompilerParams(
            dimension_semantics=("arbitrary", "arbitrary", "arbitrary"),
            vmem_limit_bytes=VMEM_LIMIT),
        name="mla_attn",
    )(qt, ka, vt)


def _swa_kernel(sink_ref, q_ref, kc_ref, kp_ref, vc_ref, vp_ref, o_ref, kf_ref, vf_ref):
    blk = SWA_BLOCK
    n_blk = q_ref.shape[1] // blk
    tile = pl.program_id(1)
    kf_ref[0:blk, :] = kp_ref[0]
    kf_ref[blk:, :] = kc_ref[0]
    vf_ref[0:blk, :] = vp_ref[0]
    vf_ref[blk:, :] = vc_ref[0]

    rows = SWA_GROUP * blk
    q_rel = lax.broadcasted_iota(jnp.int32, (rows, 2 * blk), 0) & (blk - 1)
    k_rel = lax.broadcasted_iota(jnp.int32, (rows, 2 * blk), 1)
    in_window = (k_rel > q_rel) & (k_rel <= q_rel + blk)
    head_of_row = lax.broadcasted_iota(jnp.int32, (rows, 1), 0) // blk
    lane = lax.broadcasted_iota(jnp.int32, (1, LANES), 1)
    low = lane < SWA_HEAD_DIM

    for c in range(SWA_KV_HEADS):
        sink = jnp.zeros((rows, 1), F32)
        for g in range(SWA_GROUP):
            sink = jnp.where(head_of_row == g, sink_ref[c * SWA_GROUP + g], sink)
        for n in range(n_blk):
            q = jnp.concatenate(
                [q_ref[0, n * blk:(n + 1) * blk, (c * SWA_GROUP + g) * LANES:(c * SWA_GROUP + g + 1) * LANES]
                 for g in range(SWA_GROUP)], axis=0)
            k = kf_ref[n * blk:(n + 2) * blk, c * LANES:(c + 1) * LANES]
            v = vf_ref[n * blk:(n + 2) * blk, :]
            s = _dot_nt(q, k)
            valid = in_window
            if n == 0:
                valid = valid & ((k_rel >= blk) | (tile > 0))
            s = jnp.where(valid, s, NEG_BIG)
            m = jnp.maximum(jnp.max(s, axis=-1, keepdims=True), sink)
            e = jnp.exp(s - m)
            denom = jnp.sum(e, axis=-1, keepdims=True) + jnp.exp(sink - m)
            o = _dot(e.astype(BF16), v) / denom
            for j in range(SWA_GROUP // 2):
                a = o[(2 * j) * blk:(2 * j + 1) * blk]
                bb = o[(2 * j + 1) * blk:(2 * j + 2) * blk]
                if c == 0:
                    bb = pltpu.roll(bb, SWA_HEAD_DIM, 1)
                else:
                    a = pltpu.roll(a, SWA_HEAD_DIM, 1)
                col = (c * SWA_GROUP // 2 + j) * LANES
                o_ref[0, n * blk:(n + 1) * blk, col:col + LANES] = jnp.where(low, a, bb)


def _swa_call(qb, kb, vb, sinks):
    b, s, _ = qb.shape
    tq = min(SWA_TQ, s)
    ratio = tq // SWA_BLOCK
    kw = SWA_KV_HEADS * LANES
    vw = SWA_KV_HEADS * SWA_HEAD_DIM
    prev = lambda bi, i: (bi, jnp.maximum(i * ratio - 1, 0), 0)
    cur = lambda bi, i: (bi, i, 0)
    return pl.pallas_call(
        _swa_kernel,
        grid=(b, s // tq),
        in_specs=[pl.BlockSpec(memory_space=pltpu.SMEM),
                  pl.BlockSpec((1, tq, SWA_HEADS * LANES), cur),
                  pl.BlockSpec((1, tq, kw), cur),
                  pl.BlockSpec((1, SWA_BLOCK, kw), prev),
                  pl.BlockSpec((1, tq, vw), cur),
                  pl.BlockSpec((1, SWA_BLOCK, vw), prev)],
        out_specs=pl.BlockSpec((1, tq, SWA_WIDTH), cur),
        out_shape=jax.ShapeDtypeStruct((b, s, SWA_WIDTH), F32),
        scratch_shapes=[pltpu.VMEM((tq + SWA_BLOCK, kw), BF16),
                        pltpu.VMEM((tq + SWA_BLOCK, vw), BF16)],
        compiler_params=pltpu.CompilerParams(
            dimension_semantics=("arbitrary", "arbitrary"), vmem_limit_bytes=VMEM_LIMIT),
        name="swa_attn",
    )(sinks, qb, kb, kb, vb, vb)


def _out_kernel(x_ref, oa_ref, ob_ref, ga_ref, gb_ref, wo_ref, o_ref):
    na = _rms(oa_ref[...], ga_ref[...], MLA_WIDTH).astype(BF16)
    nb = _rms(ob_ref[...], gb_ref[...], SWA_WIDTH).astype(BF16)
    y = _dot(na, wo_ref[0:MLA_WIDTH, :]) + _dot(nb, wo_ref[MLA_WIDTH:, :])
    o_ref[...] = x_ref[...] + y


def _out_call(x, oa, ob, ga, gb, wo):
    n = x.shape[0]
    tm = min(TM_ROWS, n)
    return pl.pallas_call(
        _out_kernel,
        grid=(n // tm,),
        in_specs=[_row_spec(tm, D_MODEL), _row_spec(tm, MLA_WIDTH), _row_spec(tm, SWA_WIDTH),
                  _const_spec(ga.shape), _const_spec(gb.shape), _const_spec(wo.shape)],
        out_specs=_row_spec(tm, D_MODEL),
        out_shape=jax.ShapeDtypeStruct(x.shape, F32),
        compiler_params=pltpu.CompilerParams(
            dimension_semantics=("arbitrary",), vmem_limit_bytes=VMEM_LIMIT),
        name="out_proj",
    )(x, oa, ob, ga, gb, wo)


def _pair_cols(n_heads, head_dim, base=0):
    half = head_dim // 2
    cols = []
    for pair in range(n_heads // 2):
        for part in range(2):
            for sub in range(2):
                start = base + (2 * pair + sub) * head_dim + part * half
                cols.extend(range(start, start + half))
    return np.asarray(cols, np.int32)


def _dup_cols(n_heads, head_dim, base=0):
    half = head_dim // 2
    cols = []
    for hd in range(n_heads):
        for part in range(2):
            start = base + hd * head_dim + part * half
            cols.extend(list(range(start, start + half)) * 2)
    return np.asarray(cols, np.int32)


def _layer_consts(l, mix_norm, w_in, mla_q_a_norm, mla_w_q_b, mla_kv_a_norm, mla_w_kv_b,
                  mla_q_norm, mla_k_norm, swa_q_norm, swa_k_norm):
    o_kpe = MLA_Q_RANK + MLA_KV_RANK
    o_qs = o_kpe + MLA_ROPE
    o_ks = o_qs + SWA_WIDTH
    o_vs = o_ks + SWA_KV_HEADS * SWA_HEAD_DIM
    in_cols = np.concatenate([
        np.arange(0, o_kpe, dtype=np.int32),
        _dup_cols(1, MLA_ROPE, o_kpe),
        _pair_cols(SWA_HEADS, SWA_HEAD_DIM, o_qs),
        _dup_cols(SWA_KV_HEADS, SWA_HEAD_DIM, o_ks),
        np.arange(o_vs, o_vs + SWA_KV_HEADS * SWA_HEAD_DIM, dtype=np.int32)])
    k_cols = np.concatenate(
        [np.arange(h * (MLA_NOPE + MLA_V), h * (MLA_NOPE + MLA_V) + MLA_NOPE) for h in range(MLA_HEADS)])
    v_cols = np.concatenate(
        [np.arange(h * (MLA_NOPE + MLA_V) + MLA_NOPE, (h + 1) * (MLA_NOPE + MLA_V)) for h in range(MLA_HEADS)])
    rope_gain = _dup_cols(1, MLA_ROPE, MLA_NOPE)
    swa_gain = _dup_cols(1, SWA_HEAD_DIM)
    row = lambda v: v.reshape(1, -1).astype(F32)
    return (
        row(mix_norm[l]),
        w_in[l][:, in_cols].astype(BF16),
        row(mla_q_a_norm[l]),
        mla_w_q_b[l].T.astype(BF16),
        row(mla_kv_a_norm[l]),
        mla_w_kv_b[l][:, k_cols].astype(BF16),
        mla_w_kv_b[l][:, v_cols].T.astype(BF16),
        jnp.broadcast_to(mla_q_norm[l].astype(F32)[:, None], (MLA_QK, LANES)),
        row(mla_k_norm[l][:MLA_NOPE]),
        row(mla_k_norm[l][rope_gain]),
        row(swa_q_norm[l][swa_gain]),
        row(swa_k_norm[l][swa_gain]),
    )


def _rope_tables(seq):
    pos = jnp.arange(seq, dtype=F32)
    inv = 1.0 / (ROPE_THETA ** (jnp.arange(0, 2 * HALF, 2, dtype=F32) / (2 * HALF)))
    ang = pos[:, None] * inv[None, :]
    c, s = jnp.cos(ang), jnp.sin(ang)
    return (jnp.concatenate([c, c, c, c], axis=1), jnp.concatenate([-s, -s, s, s], axis=1), c.T, s.T)


def kernel(x, ffn1_norm, ffn1_w_gate, ffn1_w_up, ffn1_w_down, mix_norm, w_in, mla_q_a_norm, mla_w_q_b, mla_kv_a_norm, mla_w_kv_b, mla_q_norm, mla_k_norm, swa_q_norm, swa_k_norm, swa_sinks, mla_out_norm, swa_out_norm, w_o, ffn2_norm, ffn2_w_gate, ffn2_w_up, ffn2_w_down):
    b, s, d = x.shape
    depth = w_in.shape[0]
    tables = _rope_tables(s)
    row = lambda v: v.reshape(1, -1).astype(F32)
    xf = x.reshape(b * s, d)
    for l in range(depth):
        xf = _ffn_call(xf, row(ffn1_norm[l]), ffn1_w_gate[l].astype(BF16),
                       ffn1_w_up[l].astype(BF16), ffn1_w_down[l].astype(BF16))
        consts = _layer_consts(l, mix_norm, w_in, mla_q_a_norm, mla_w_q_b, mla_kv_a_norm,
                               mla_w_kv_b, mla_q_norm, mla_k_norm, swa_q_norm, swa_k_norm)
        qt, ka, vt, qb, kb, vb = _proj_call(xf, s, consts, tables)
        shp = lambda t: t.reshape(b, s, t.shape[-1])
        tiles = lambda t: t.reshape(b, -1, t.shape[-2], t.shape[-1])
        oa = _mla_call(tiles(qt), shp(ka), tiles(vt))
        ob = _swa_call(shp(qb), shp(kb), shp(vb), swa_sinks[l].astype(F32))
        xf = _out_call(xf, oa.reshape(b * s, MLA_WIDTH), ob.reshape(b * s, SWA_WIDTH),
                       row(mla_out_norm[l]), row(swa_out_norm[l]), w_o[l].astype(BF16))
        xf = _ffn_call(xf, row(ffn2_norm[l]), ffn2_w_gate[l].astype(BF16),
                       ffn2_w_up[l].astype(BF16), ffn2_w_down[l].astype(BF16))
    return xf.reshape(b, s, d)
```

```python
import functools

import numpy as np
import jax
import jax.numpy as jnp
from jax import lax
from jax.experimental import pallas as pl
from jax.experimental.pallas import tpu as pltpu

D_MODEL = 1024
EPS = 1e-6
ROPE_THETA = 10000.0
MLA_HEADS = 4
MLA_Q_RANK = 256
MLA_KV_RANK = 128
MLA_NOPE = 128
MLA_ROPE = 64
MLA_V = 128
MLA_QK = MLA_NOPE + MLA_ROPE
MLA_WIDTH = MLA_HEADS * MLA_V
SWA_HEADS = 8
SWA_KV_HEADS = 2
SWA_GROUP = SWA_HEADS // SWA_KV_HEADS
SWA_HEAD_DIM = 64
SWA_BLOCK = 128
SWA_WIDTH = SWA_HEADS * SWA_HEAD_DIM
D_FF = 2816

LANES = 128
HALF = SWA_HEAD_DIM // 2
MLA_QK_PAD = 2 * LANES
BF16_ROWS = 16
MLA_V_ROWS = MLA_V + BF16_ROWS
NEG_BIG = -1e30

VMEM_LIMIT = 56 * 1024 * 1024

LOG2E = 1.4426950408889634

TM_ROWS = 512
SWA_TQ = 512

BF16 = jnp.bfloat16
F32 = jnp.float32


def _rms(t, gain, width):
    ss = jnp.sum(t * t, axis=-1, keepdims=True)
    return t * lax.rsqrt(ss * (1.0 / width) + EPS) * gain


def _dot(a, b):
    return jnp.dot(a, b, preferred_element_type=F32)


def _dot_nt(a, b):
    return lax.dot_general(a, b, (((1,), (1,)), ((), ())), preferred_element_type=F32)


def _ffn_block(x, g, wg_ref, wu_ref, wd_ref):
    h = _rms(x, g, D_MODEL).astype(BF16)
    gate = _dot(h, wg_ref[...])
    up = _dot(h, wu_ref[...])
    act = (gate * jax.nn.sigmoid(gate) * up).astype(BF16)
    return x + 0.5 * _dot(act, wd_ref[...])


def _ffn_kernel(x_ref, g_ref, wg_ref, wu_ref, wd_ref, o_ref):
    o_ref[...] = _ffn_block(x_ref[...], g_ref[...], wg_ref, wu_ref, wd_ref)


def _const_spec(shape):
    nd = len(shape)
    return pl.BlockSpec(shape, lambda *_: (0,) * nd, pipeline_mode=pl.Buffered(1))


def _row_spec(tm, width):
    return pl.BlockSpec((tm, width), lambda i: (i, 0))


def _ffn_call(x, g, wg, wu, wd):
    n = x.shape[0]
    tm = min(TM_ROWS, n)
    return pl.pallas_call(
        _ffn_kernel,
        grid=(n // tm,),
        in_specs=[_row_spec(tm, D_MODEL), _const_spec(g.shape), _const_spec(wg.shape),
                  _const_spec(wu.shape), _const_spec(wd.shape)],
        out_specs=_row_spec(tm, D_MODEL),
        out_shape=jax.ShapeDtypeStruct(x.shape, F32),
        compiler_params=pltpu.CompilerParams(
            dimension_semantics=("arbitrary",), vmem_limit_bytes=VMEM_LIMIT),
        name="ffn",
    )(x, g, wg, wu, wd)


C_CQ = 0
C_CKV = C_CQ + MLA_Q_RANK
C_KPE = C_CKV + MLA_KV_RANK
C_QS = C_KPE + LANES
C_KS = C_QS + SWA_HEADS // 2 * LANES
C_VS = C_KS + SWA_KV_HEADS * LANES
C_END = C_VS + SWA_KV_HEADS * SWA_HEAD_DIM


def _rope(t, cos, sin_signed):
    return t * cos + pltpu.roll(t, 2 * HALF, 1) * sin_signed


def _proj_kernel(x_ref, g_ref, win_ref, gqa_ref, wqbt_ref, gkva_ref, wkb_ref, wvt_ref,
                 gq_ref, gkn_ref, gkr_ref, gsq_ref, gsk_ref, cos_ref, sin_ref, cost_ref, sint_ref,
                 qt_ref, ka_ref, vt_ref, qb_ref, kb_ref, vb_ref):
    tm = x_ref.shape[0]
    q_scale = MLA_QK ** -0.5 * LOG2E
    swa_scale = SWA_HEAD_DIM ** -0.5
    cos = cos_ref[...]
    sin = sin_ref[...]
    lane = lax.broadcasted_iota(jnp.int32, (1, LANES), 1)
    first = (lane & (2 * HALF - 1)) < HALF
    mask = (first.astype(F32), 1.0 - first.astype(F32))

    h = _rms(x_ref[...], g_ref[...], D_MODEL).astype(BF16)
    proj = _dot(h, win_ref[...])

    cq_t = _rms(proj[:, C_CQ:C_CKV], gqa_ref[...], MLA_Q_RANK).T.astype(BF16)
    q_up = _dot(wqbt_ref[...], cq_t)
    gq = jnp.tile(gq_ref[...], (1, tm // LANES))
    cos_t = cost_ref[...]
    sin_t = sint_ref[...]
    zeros = jnp.zeros((HALF, tm), BF16)
    for hd in range(MLA_HEADS):
        blk = q_up[hd * MLA_QK:(hd + 1) * MLA_QK]
        ss = jnp.sum(blk * blk, axis=0, keepdims=True)
        qn = blk * (lax.rsqrt(ss * (1.0 / MLA_QK) + EPS) * q_scale) * gq
        t1 = qn[MLA_NOPE:MLA_NOPE + HALF]
        t2 = qn[MLA_NOPE + HALF:]
        base = hd * MLA_QK_PAD
        qt_ref[0, base:base + MLA_NOPE, :] = qn[:MLA_NOPE].astype(BF16)
        qt_ref[0, base + MLA_NOPE:base + MLA_NOPE + HALF, :] = (t1 * cos_t - t2 * sin_t).astype(BF16)
        qt_ref[0, base + MLA_NOPE + HALF:base + MLA_NOPE + 2 * HALF, :] = zeros
        qt_ref[0, base + MLA_NOPE + 2 * HALF:base + MLA_NOPE + 3 * HALF, :] = (
            t2 * cos_t + t1 * sin_t).astype(BF16)
        qt_ref[0, base + MLA_NOPE + 3 * HALF:base + MLA_QK_PAD, :] = zeros

    ckv = _rms(proj[:, C_CKV:C_KPE], gkva_ref[...], MLA_KV_RANK)
    k_up = _dot(ckv.astype(BF16), wkb_ref[...])
    v_t = _dot(wvt_ref[...], ckv.T.astype(BF16)).astype(BF16)
    ones_row = (lax.broadcasted_iota(jnp.int32, (BF16_ROWS, tm), 0) == 0).astype(BF16)
    for hd in range(MLA_HEADS):
        vt_ref[0, hd * MLA_V_ROWS:hd * MLA_V_ROWS + MLA_V, :] = v_t[hd * MLA_V:(hd + 1) * MLA_V]
        vt_ref[0, hd * MLA_V_ROWS + MLA_V:(hd + 1) * MLA_V_ROWS, :] = ones_row
    kpe = proj[:, C_KPE:C_QS]
    ss_pe = 0.5 * jnp.sum(kpe * kpe, axis=-1, keepdims=True)
    kpe_rot = _rope(kpe * gkr_ref[...], cos, sin)
    for hd in range(MLA_HEADS):
        nope = k_up[:, hd * MLA_NOPE:(hd + 1) * MLA_NOPE]
        ss = jnp.sum(nope * nope, axis=-1, keepdims=True) + ss_pe
        r = lax.rsqrt(ss * (1.0 / MLA_QK) + EPS)
        ka_ref[:, hd * MLA_QK_PAD: hd * MLA_QK_PAD + MLA_NOPE] = (
            nope * r * gkn_ref[...]).astype(BF16)
        ka_ref[:, hd * MLA_QK_PAD + MLA_NOPE:(hd + 1) * MLA_QK_PAD] = (kpe_rot * r).astype(BF16)

    for pair in range(SWA_HEADS // 2):
        xq = proj[:, C_QS + pair * LANES: C_QS + (pair + 1) * LANES]
        xq2 = xq * xq
        inv = [lax.rsqrt(jnp.sum(xq2 * mask[sub], axis=-1, keepdims=True) * (1.0 / SWA_HEAD_DIM) + EPS)
               for sub in range(2)]
        xs = xq * jnp.where(first, inv[0], inv[1]) * gsq_ref[...]
        xrot = _rope(xs, cos, sin) * swa_scale
        for sub in range(2):
            hd = 2 * pair + sub
            qb_ref[:, hd * LANES:(hd + 1) * LANES] = (xrot * mask[sub]).astype(BF16)
    for c in range(SWA_KV_HEADS):
        xk = proj[:, C_KS + c * LANES: C_KS + (c + 1) * LANES]
        ss = 0.5 * jnp.sum(xk * xk, axis=-1, keepdims=True)
        r = lax.rsqrt(ss * (1.0 / SWA_HEAD_DIM) + EPS)
        kb_ref[:, c * LANES:(c + 1) * LANES] = _rope(xk * r * gsk_ref[...], cos, sin).astype(BF16)
    vb_ref[...] = proj[:, C_VS:C_END].astype(BF16)


def _proj_call(x, seq, consts, tables):
    n = x.shape[0]
    tm = min(TM_ROWS, seq)
    n_seq_tiles = seq // tm
    pos_spec = pl.BlockSpec((tm, LANES), lambda i: (i % n_seq_tiles, 0))
    pos_t_spec = pl.BlockSpec((HALF, tm), lambda i: (0, i % n_seq_tiles))
    row_widths = {1: MLA_HEADS * MLA_QK_PAD, 3: SWA_HEADS * LANES, 4: SWA_KV_HEADS * LANES,
                  5: SWA_KV_HEADS * SWA_HEAD_DIM}
    col_heights = {0: MLA_HEADS * MLA_QK_PAD, 2: MLA_HEADS * MLA_V_ROWS}
    out_specs, out_shape = [], []
    for idx in range(6):
        if idx in row_widths:
            out_specs.append(_row_spec(tm, row_widths[idx]))
            out_shape.append(jax.ShapeDtypeStruct((n, row_widths[idx]), BF16))
        else:
            out_specs.append(pl.BlockSpec((1, col_heights[idx], tm), lambda i: (i, 0, 0)))
            out_shape.append(jax.ShapeDtypeStruct((n // tm, col_heights[idx], tm), BF16))
    return pl.pallas_call(
        _proj_kernel,
        grid=(n // tm,),
        in_specs=([_row_spec(tm, D_MODEL)] + [_const_spec(c.shape) for c in consts]
                  + [pos_spec, pos_spec, pos_t_spec, pos_t_spec]),
        out_specs=out_specs,
        out_shape=out_shape,
        compiler_params=pltpu.CompilerParams(
            dimension_semantics=("arbitrary",), vmem_limit_bytes=VMEM_LIMIT),
        name="proj",
    )(x, *consts, *tables)


MLA_STREAMS = 4


def _mla_kernel(q_ref, k_ref, v_ref, o_ref, s_ref, p_ref, acc_ref):
    t = q_ref.shape[3]
    tk = t // 2
    qi = pl.program_id(2)
    heads = range(MLA_STREAMS)

    def scores(tile, half, h):
        start = pl.multiple_of(tile * t, t) + half * tk
        k = k_ref[0, pl.ds(start, tk), h * MLA_QK_PAD:(h + 1) * MLA_QK_PAD]
        return _dot(k, q_ref[0, 0, h * MLA_QK_PAD:(h + 1) * MLA_QK_PAD, :])

    def weighted_values(tile, half, h, buf):
        v = v_ref[0, tile, h * MLA_V_ROWS:(h + 1) * MLA_V_ROWS, half * tk:(half + 1) * tk]
        return _dot(v, p_ref[buf, h, :, :t])

    def phase(cur, state, nxt_tile, prev_tile, mask_half=None):
        new_state = []
        for h in heads:
            m, alpha_prev = state[h]
            if prev_tile is not None:
                pv = weighted_values(prev_tile, 1 - cur, h, 1 - cur)
            if nxt_tile is not None:
                s_ref[1 - cur, h, :, :t] = scores(nxt_tile, 1 - cur, h)
            s = s_ref[cur, h, :, :t]
            if mask_half is not None:
                key = lax.broadcasted_iota(jnp.int32, (tk, t), 0) + mask_half * tk
                qry = lax.broadcasted_iota(jnp.int32, (tk, t), 1)
                s = jnp.where(key <= qry, s, NEG_BIG)
            m_new = jnp.maximum(m, jnp.max(s, axis=0, keepdims=True))
            p = jnp.exp2(s - m_new)
            p_ref[cur, h, :, :t] = p.astype(BF16)
            alpha = jnp.exp2(m - m_new)
            if prev_tile is not None:
                acc_ref[h, :, :t] = alpha_prev * acc_ref[h, :, :t] + pv
            new_state.append((m_new, alpha))
        return tuple(new_state)

    state = tuple((jnp.full((1, t), NEG_BIG, F32), jnp.zeros((1, t), F32)) for _ in heads)
    for h in heads:
        acc_ref[h, :, :t] = jnp.zeros((MLA_V_ROWS, t), F32)
        s_ref[0, h, :, :t] = scores(qi, 0, h)
    state = phase(0, state, qi, None, mask_half=0)
    state = phase(1, state, 0, qi, mask_half=1)

    def body(jj, state):
        prev_tile = jnp.where(jj == 0, qi, jj - 1)
        state = phase(0, state, jj, prev_tile)
        return phase(1, state, jj + 1, jj)

    state = lax.fori_loop(0, qi, body, state)
    last_tile = jnp.maximum(qi - 1, 0)
    for h in heads:
        m, alpha_prev = state[h]
        acc = alpha_prev * acc_ref[h, :, :t] + weighted_values(last_tile, 1, h, 1)
        o_ref[0, :, h * MLA_V:(h + 1) * MLA_V] = (acc[:MLA_V] / acc[MLA_V:MLA_V + 1]).T


def _mla_call(qt, ka, vt):
    b, nq, _, t = qt.shape
    s = ka.shape[1]
    qw = MLA_STREAMS * MLA_QK_PAD
    vw = MLA_STREAMS * MLA_V_ROWS
    ow = MLA_STREAMS * MLA_V
    return pl.pallas_call(
        _mla_kernel,
        grid=(b, MLA_HEADS // MLA_STREAMS, nq),
        in_specs=[pl.BlockSpec((1, 1, qw, t), lambda bi, h, i: (bi, i, h, 0)),
                  pl.BlockSpec((1, s, qw), lambda bi, h, i: (bi, 0, h), pipeline_mode=pl.Buffered(1)),
                  pl.BlockSpec((1, nq, vw, t), lambda bi, h, i: (bi, 0, h, 0),
                               pipeline_mode=pl.Buffered(1))],
        out_specs=pl.BlockSpec((1, t, ow), lambda bi, h, i: (bi, i, h)),
        out_shape=jax.ShapeDtypeStruct((b, s, MLA_WIDTH), F32),
        scratch_shapes=[pltpu.VMEM((2, MLA_STREAMS, t // 2, t + LANES), F32),
                        pltpu.VMEM((2, MLA_STREAMS, t // 2, t + LANES), BF16),
                        pltpu.VMEM((MLA_STREAMS, MLA_V_ROWS, t + LANES), F32)],
        compiler_params=pltpu.CompilerParams(
            dimension_semantics=("arbitrary", "arbitrary", "arbitrary"),
            vmem_limit_bytes=VMEM_LIMIT),
        name="mla_attn",
    )(qt, ka, vt)


def _swa_kernel(sink_ref, q_ref, kc_ref, kp_ref, vc_ref, vp_ref, o_ref, kf_ref, vf_ref):
    blk = SWA_BLOCK
    n_blk = q_ref.shape[1] // blk
    tile = pl.program_id(1)
    kf_ref[0:blk, :] = kp_ref[0]
    kf_ref[blk:, :] = kc_ref[0]
    vf_ref[0:blk, :] = vp_ref[0]
    vf_ref[blk:, :] = vc_ref[0]

    rows = SWA_GROUP * blk
    q_rel = lax.broadcasted_iota(jnp.int32, (rows, 2 * blk), 0) & (blk - 1)
    k_rel = lax.broadcasted_iota(jnp.int32, (rows, 2 * blk), 1)
    in_window = (k_rel > q_rel) & (k_rel <= q_rel + blk)
    head_of_row = lax.broadcasted_iota(jnp.int32, (rows, 1), 0) // blk
    lane = lax.broadcasted_iota(jnp.int32, (1, LANES), 1)
    low = lane < SWA_HEAD_DIM

    for c in range(SWA_KV_HEADS):
        sink = jnp.zeros((rows, 1), F32)
        for g in range(SWA_GROUP):
            sink = jnp.where(head_of_row == g, sink_ref[c * SWA_GROUP + g], sink)
        for n in range(n_blk):
            q = jnp.concatenate(
                [q_ref[0, n * blk:(n + 1) * blk, (c * SWA_GROUP + g) * LANES:(c * SWA_GROUP + g + 1) * LANES]
                 for g in range(SWA_GROUP)], axis=0)
            k = kf_ref[n * blk:(n + 2) * blk, c * LANES:(c + 1) * LANES]
            v = vf_ref[n * blk:(n + 2) * blk, :]
            s = _dot_nt(q, k)
            valid = in_window
            if n == 0:
                valid = valid & ((k_rel >= blk) | (tile > 0))
            s = jnp.where(valid, s, NEG_BIG)
            m = jnp.maximum(jnp.max(s, axis=-1, keepdims=True), sink)
            e = jnp.exp(s - m)
            denom = jnp.sum(e, axis=-1, keepdims=True) + jnp.exp(sink - m)
            o = _dot(e.astype(BF16), v) / denom
            for j in range(SWA_GROUP // 2):
                a = o[(2 * j) * blk:(2 * j + 1) * blk]
                bb = o[(2 * j + 1) * blk:(2 * j + 2) * blk]
                if c == 0:
                    bb = pltpu.roll(bb, SWA_HEAD_DIM, 1)
                else:
                    a = pltpu.roll(a, SWA_HEAD_DIM, 1)
                col = (c * SWA_GROUP // 2 + j) * LANES
                o_ref[0, n * blk:(n + 1) * blk, col:col + LANES] = jnp.where(low, a, bb)


def _swa_call(qb, kb, vb, sinks):
    b, s, _ = qb.shape
    tq = min(SWA_TQ, s)
    ratio = tq // SWA_BLOCK
    kw = SWA_KV_HEADS * LANES
    vw = SWA_KV_HEADS * SWA_HEAD_DIM
    prev = lambda bi, i: (bi, jnp.maximum(i * ratio - 1, 0), 0)
    cur = lambda bi, i: (bi, i, 0)
    return pl.pallas_call(
        _swa_kernel,
        grid=(b, s // tq),
        in_specs=[pl.BlockSpec(memory_space=pltpu.SMEM),
                  pl.BlockSpec((1, tq, SWA_HEADS * LANES), cur),
                  pl.BlockSpec((1, tq, kw), cur),
                  pl.BlockSpec((1, SWA_BLOCK, kw), prev),
                  pl.BlockSpec((1, tq, vw), cur),
                  pl.BlockSpec((1, SWA_BLOCK, vw), prev)],
        out_specs=pl.BlockSpec((1, tq, SWA_WIDTH), cur),
        out_shape=jax.ShapeDtypeStruct((b, s, SWA_WIDTH), F32),
        scratch_shapes=[pltpu.VMEM((tq + SWA_BLOCK, kw), BF16),
                        pltpu.VMEM((tq + SWA_BLOCK, vw), BF16)],
        compiler_params=pltpu.CompilerParams(
            dimension_semantics=("arbitrary", "arbitrary"), vmem_limit_bytes=VMEM_LIMIT),
        name="swa_attn",
    )(sinks, qb, kb, kb, vb, vb)


def _out_kernel(x_ref, oa_ref, ob_ref, ga_ref, gb_ref, wo_ref, o_ref):
    na = _rms(oa_ref[...], ga_ref[...], MLA_WIDTH).astype(BF16)
    nb = _rms(ob_ref[...], gb_ref[...], SWA_WIDTH).astype(BF16)
    y = _dot(na, wo_ref[0:MLA_WIDTH, :]) + _dot(nb, wo_ref[MLA_WIDTH:, :])
    o_ref[...] = x_ref[...] + y


def _out_call(x, oa, ob, ga, gb, wo):
    n = x.shape[0]
    tm = min(TM_ROWS, n)
    return pl.pallas_call(
        _out_kernel,
        grid=(n // tm,),
        in_specs=[_row_spec(tm, D_MODEL), _row_spec(tm, MLA_WIDTH), _row_spec(tm, SWA_WIDTH),
                  _const_spec(ga.shape), _const_spec(gb.shape), _const_spec(wo.shape)],
        out_specs=_row_spec(tm, D_MODEL),
        out_shape=jax.ShapeDtypeStruct(x.shape, F32),
        compiler_params=pltpu.CompilerParams(
            dimension_semantics=("arbitrary",), vmem_limit_bytes=VMEM_LIMIT),
        name="out_proj",
    )(x, oa, ob, ga, gb, wo)


def _pair_cols(n_heads, head_dim, base=0):
    half = head_dim // 2
    cols = []
    for pair in range(n_heads // 2):
        for part in range(2):
            for sub in range(2):
                start = base + (2 * pair + sub) * head_dim + part * half
                cols.extend(range(start, start + half))
    return np.asarray(cols, np.int32)


def _dup_cols(n_heads, head_dim, base=0):
    half = head_dim // 2
    cols = []
    for hd in range(n_heads):
        for part in range(2):
            start = base + hd * head_dim + part * half
            cols.extend(list(range(start, start + half)) * 2)
    return np.asarray(cols, np.int32)


def _layer_consts(l, mix_norm, w_in, mla_q_a_norm, mla_w_q_b, mla_kv_a_norm, mla_w_kv_b,
                  mla_q_norm, mla_k_norm, swa_q_norm, swa_k_norm):
    o_kpe = MLA_Q_RANK + MLA_KV_RANK
    o_qs = o_kpe + MLA_ROPE
    o_ks = o_qs + SWA_WIDTH
    o_vs = o_ks + SWA_KV_HEADS * SWA_HEAD_DIM
    in_cols = np.concatenate([
        np.arange(0, o_kpe, dtype=np.int32),
        _dup_cols(1, MLA_ROPE, o_kpe),
        _pair_cols(SWA_HEADS, SWA_HEAD_DIM, o_qs),
        _dup_cols(SWA_KV_HEADS, SWA_HEAD_DIM, o_ks),
        np.arange(o_vs, o_vs + SWA_KV_HEADS * SWA_HEAD_DIM, dtype=np.int32)])
    k_cols = np.concatenate(
        [np.arange(h * (MLA_NOPE + MLA_V), h * (MLA_NOPE + MLA_V) + MLA_NOPE) for h in range(MLA_HEADS)])
    v_cols = np.concatenate(
        [np.arange(h * (MLA_NOPE + MLA_V) + MLA_NOPE, (h + 1) * (MLA_NOPE + MLA_V)) for h in range(MLA_HEADS)])
    rope_gain = _dup_cols(1, MLA_ROPE, MLA_NOPE)
    swa_gain = _dup_cols(1, SWA_HEAD_DIM)
    row = lambda v: v.reshape(1, -1).astype(F32)
    return (
        row(mix_norm[l]),
        w_in[l][:, in_cols].astype(BF16),
        row(mla_q_a_norm[l]),
        mla_w_q_b[l].T.astype(BF16),
        row(mla_kv_a_norm[l]),
        mla_w_kv_b[l][:, k_cols].astype(BF16),
        mla_w_kv_b[l][:, v_cols].T.astype(BF16),
        jnp.broadcast_to(mla_q_norm[l].astype(F32)[:, None], (MLA_QK, LANES)),
        row(mla_k_norm[l][:MLA_NOPE]),
        row(mla_k_norm[l][rope_gain]),
        row(swa_q_norm[l][swa_gain]),
        row(swa_k_norm[l][swa_gain]),
    )


def _rope_tables(seq):
    pos = jnp.arange(seq, dtype=F32)
    inv = 1.0 / (ROPE_THETA ** (jnp.arange(0, 2 * HALF, 2, dtype=F32) / (2 * HALF)))
    ang = pos[:, None] * inv[None, :]
    c, s = jnp.cos(ang), jnp.sin(ang)
    return (jnp.concatenate([c, c, c, c], axis=1), jnp.concatenate([-s, -s, s, s], axis=1), c.T, s.T)


def kernel(x, ffn1_norm, ffn1_w_gate, ffn1_w_up, ffn1_w_down, mix_norm, w_in, mla_q_a_norm, mla_w_q_b, mla_kv_a_norm, mla_w_kv_b, mla_q_norm, mla_k_norm, swa_q_norm, swa_k_norm, swa_sinks, mla_out_norm, swa_out_norm, w_o, ffn2_norm, ffn2_w_gate, ffn2_w_up, ffn2_w_down):
    b, s, d = x.shape
    depth = w_in.shape[0]
    tables = _rope_tables(s)
    row = lambda v: v.reshape(1, -1).astype(F32)
    xf = x.reshape(b * s, d)
    for l in range(depth):
        xf = _ffn_call(xf, row(ffn1_norm[l]), ffn1_w_gate[l].astype(BF16),
                       ffn1_w_up[l].astype(BF16), ffn1_w_down[l].astype(BF16))
        consts = _layer_consts(l, mix_norm, w_in, mla_q_a_norm, mla_w_q_b, mla_kv_a_norm,
                               mla_w_kv_b, mla_q_norm, mla_k_norm, swa_q_norm, swa_k_norm)
        qt, ka, vt, qb, kb, vb = _proj_call(xf, s, consts, tables)
        shp = lambda t: t.reshape(b, s, t.shape[-1])
        tiles = lambda t: t.reshape(b, -1, t.shape[-2], t.shape[-1])
        oa = _mla_call(tiles(qt), shp(ka), tiles(vt))
        ob = _swa_call(shp(qb), shp(kb), shp(vb), swa_sinks[l].astype(F32))
        xf = _out_call(xf, oa.reshape(b * s, MLA_WIDTH), ob.reshape(b * s, SWA_WIDTH),
                       row(mla_out_norm[l]), row(swa_out_norm[l]), w_o[l].astype(BF16))
        xf = _ffn_call(xf, row(ffn2_norm[l]), ffn2_w_gate[l].astype(BF16),
                       ffn2_w_up[l].astype(BF16), ffn2_w_down[l].astype(BF16))
    return xf.reshape(b, s, d)
```

```python
import functools

import numpy as np
import jax
import jax.numpy as jnp
from jax import lax
from jax.experimental import pallas as pl
from jax.experimental.pallas import tpu as pltpu

D_MODEL = 1024
EPS = 1e-6
ROPE_THETA = 10000.0
MLA_HEADS = 4
MLA_Q_RANK = 256
MLA_KV_RANK = 128
MLA_NOPE = 128
MLA_ROPE = 64
MLA_V = 128
MLA_QK = MLA_NOPE + MLA_ROPE
MLA_WIDTH = MLA_HEADS * MLA_V
SWA_HEADS = 8
SWA_KV_HEADS = 2
SWA_GROUP = SWA_HEADS // SWA_KV_HEADS
SWA_HEAD_DIM = 64
SWA_BLOCK = 128
SWA_WIDTH = SWA_HEADS * SWA_HEAD_DIM
D_FF = 2816

LANES = 128
HALF = SWA_HEAD_DIM // 2
MLA_QK_PAD = 2 * LANES
BF16_ROWS = 16
MLA_V_ROWS = MLA_V + BF16_ROWS
NEG_BIG = -1e30

VMEM_LIMIT = 56 * 1024 * 1024

LOG2E = 1.4426950408889634

TM_ROWS = 512
SWA_TQ = 512

BF16 = jnp.bfloat16
F32 = jnp.float32


def _rms(t, gain, width):
    ss = jnp.sum(t * t, axis=-1, keepdims=True)
    return t * lax.rsqrt(ss * (1.0 / width) + EPS) * gain


def _dot(a, b):
    return jnp.dot(a, b, preferred_element_type=F32)


def _dot_nt(a, b):
    return lax.dot_general(a, b, (((1,), (1,)), ((), ())), preferred_element_type=F32)


def _ffn_block(x, g, wg_ref, wu_ref, wd_ref):
    h = _rms(x, g, D_MODEL).astype(BF16)
    gate = _dot(h, wg_ref[...])
    up = _dot(h, wu_ref[...])
    act = (gate * jax.nn.sigmoid(gate) * up).astype(BF16)
    return x + 0.5 * _dot(act, wd_ref[...])


def _ffn_kernel(x_ref, g_ref, wg_ref, wu_ref, wd_ref, o_ref):
    o_ref[...] = _ffn_block(x_ref[...], g_ref[...], wg_ref, wu_ref, wd_ref)


def _const_spec(shape):
    nd = len(shape)
    return pl.BlockSpec(shape, lambda *_: (0,) * nd, pipeline_mode=pl.Buffered(1))


def _row_spec(tm, width):
    return pl.BlockSpec((tm, width), lambda i: (i, 0))


def _ffn_call(x, g, wg, wu, wd):
    n = x.shape[0]
    tm = min(TM_ROWS, n)
    return pl.pallas_call(
        _ffn_kernel,
        grid=(n // tm,),
        in_specs=[_row_spec(tm, D_MODEL), _const_spec(g.shape), _const_spec(wg.shape),
                  _const_spec(wu.shape), _const_spec(wd.shape)],
        out_specs=_row_spec(tm, D_MODEL),
        out_shape=jax.ShapeDtypeStruct(x.shape, F32),
        compiler_params=pltpu.CompilerParams(
            dimension_semantics=("arbitrary",), vmem_limit_bytes=VMEM_LIMIT),
        name="ffn",
    )(x, g, wg, wu, wd)


C_CQ = 0
C_CKV = C_CQ + MLA_Q_RANK
C_KPE = C_CKV + MLA_KV_RANK
C_QS = C_KPE + LANES
C_KS = C_QS + SWA_HEADS // 2 * LANES
C_VS = C_KS + SWA_KV_HEADS * LANES
C_END = C_VS + SWA_KV_HEADS * SWA_HEAD_DIM


def _rope(t, cos, sin_signed):
    return t * cos + pltpu.roll(t, 2 * HALF, 1) * sin_signed


def _proj_body(x, g_ref, win_ref, gqa_ref, wqbt_ref, gkva_ref, wkb_ref, wvt_ref,
               gq_ref, gkn_ref, gkr_ref, gsq_ref, gsk_ref, cos_ref, sin_ref, cost_ref, sint_ref,
               qt_ref, ka_ref, vt_ref, qb_ref, kb_ref, vb_ref):
    tm = x.shape[0]
    q_scale = MLA_QK ** -0.5 * LOG2E
    swa_scale = SWA_HEAD_DIM ** -0.5
    cos = cos_ref[...]
    sin = sin_ref[...]
    lane = lax.broadcasted_iota(jnp.int32, (1, LANES), 1)
    first = (lane & (2 * HALF - 1)) < HALF
    mask = (first.astype(F32), 1.0 - first.astype(F32))

    h = _rms(x, g_ref[...], D_MODEL).astype(BF16)
    proj = _dot(h, win_ref[...])

    cq_t = _rms(proj[:, C_CQ:C_CKV], gqa_ref[...], MLA_Q_RANK).T.astype(BF16)
    q_up = _dot(wqbt_ref[...], cq_t)
    gq = jnp.tile(gq_ref[...], (1, tm // LANES))
    cos_t = cost_ref[...]
    sin_t = sint_ref[...]
    zeros = jnp.zeros((HALF, tm), BF16)
    for hd in range(MLA_HEADS):
        blk = q_up[hd * MLA_QK:(hd + 1) * MLA_QK]
        ss = jnp.sum(blk * blk, axis=0, keepdims=True)
        qn = blk * (lax.rsqrt(ss * (1.0 / MLA_QK) + EPS) * q_scale) * gq
        t1 = qn[MLA_NOPE:MLA_NOPE + HALF]
        t2 = qn[MLA_NOPE + HALF:]
        base = hd * MLA_QK_PAD
        qt_ref[0, base:base + MLA_NOPE, :] = qn[:MLA_NOPE].astype(BF16)
        qt_ref[0, base + MLA_NOPE:base + MLA_NOPE + HALF, :] = (t1 * cos_t - t2 * sin_t).astype(BF16)
        qt_ref[0, base + MLA_NOPE + HALF:base + MLA_NOPE + 2 * HALF, :] = zeros
        qt_ref[0, base + MLA_NOPE + 2 * HALF:base + MLA_NOPE + 3 * HALF, :] = (
            t2 * cos_t + t1 * sin_t).astype(BF16)
        qt_ref[0, base + MLA_NOPE + 3 * HALF:base + MLA_QK_PAD, :] = zeros

    ckv = _rms(proj[:, C_CKV:C_KPE], gkva_ref[...], MLA_KV_RANK)
    k_up = _dot(ckv.astype(BF16), wkb_ref[...])
    v_t = _dot(wvt_ref[...], ckv.T.astype(BF16)).astype(BF16)
    ones_row = (lax.broadcasted_iota(jnp.int32, (BF16_ROWS, tm), 0) == 0).astype(BF16)
    for hd in range(MLA_HEADS):
        vt_ref[0, hd * MLA_V_ROWS:hd * MLA_V_ROWS + MLA_V, :] = v_t[hd * MLA_V:(hd + 1) * MLA_V]
        vt_ref[0, hd * MLA_V_ROWS + MLA_V:(hd + 1) * MLA_V_ROWS, :] = ones_row
    kpe = proj[:, C_KPE:C_QS]
    ss_pe = 0.5 * jnp.sum(kpe * kpe, axis=-1, keepdims=True)
    kpe_rot = _rope(kpe * gkr_ref[...], cos, sin)
    for hd in range(MLA_HEADS):
        nope = k_up[:, hd * MLA_NOPE:(hd + 1) * MLA_NOPE]
        ss = jnp.sum(nope * nope, axis=-1, keepdims=True) + ss_pe
        r = lax.rsqrt(ss * (1.0 / MLA_QK) + EPS)
        ka_ref[:, hd * MLA_QK_PAD: hd * MLA_QK_PAD + MLA_NOPE] = (
            nope * r * gkn_ref[...]).astype(BF16)
        ka_ref[:, hd * MLA_QK_PAD + MLA_NOPE:(hd + 1) * MLA_QK_PAD] = (kpe_rot * r).astype(BF16)

    for pair in range(SWA_HEADS // 2):
        xq = proj[:, C_QS + pair * LANES: C_QS + (pair + 1) * LANES]
        xq2 = xq * xq
        inv = [lax.rsqrt(jnp.sum(xq2 * mask[sub], axis=-1, keepdims=True) * (1.0 / SWA_HEAD_DIM) + EPS)
               for sub in range(2)]
        xs = xq * jnp.where(first, inv[0], inv[1]) * gsq_ref[...]
        xrot = _rope(xs, cos, sin) * swa_scale
        for sub in range(2):
            hd = 2 * pair + sub
            qb_ref[:, hd * LANES:(hd + 1) * LANES] = (xrot * mask[sub]).astype(BF16)
    for c in range(SWA_KV_HEADS):
        xk = proj[:, C_KS + c * LANES: C_KS + (c + 1) * LANES]
        ss = 0.5 * jnp.sum(xk * xk, axis=-1, keepdims=True)
        r = lax.rsqrt(ss * (1.0 / SWA_HEAD_DIM) + EPS)
        kb_ref[:, c * LANES:(c + 1) * LANES] = _rope(xk * r * gsk_ref[...], cos, sin).astype(BF16)
    vb_ref[...] = proj[:, C_VS:C_END].astype(BF16)


N_FFN_CONSTS = 4
N_PROJ_CONSTS = 12
N_PROJ_TABLES = 4


def _ffn_proj_kernel(*refs):
    x_ref = refs[0]
    ffn_refs = refs[1:1 + N_FFN_CONSTS]
    proj_refs = refs[1 + N_FFN_CONSTS:1 + N_FFN_CONSTS + N_PROJ_CONSTS + N_PROJ_TABLES]
    o_ref = refs[1 + N_FFN_CONSTS + N_PROJ_CONSTS + N_PROJ_TABLES]
    proj_out_refs = refs[-7:-1]
    x_prev = refs[-1]

    @pl.when(pl.program_id(0) == 0)
    def _():
        x_prev[...] = jnp.zeros_like(x_prev)

    _proj_body(x_prev[...], *proj_refs, *proj_out_refs)
    g_ref, wg_ref, wu_ref, wd_ref = ffn_refs
    x_new = _ffn_block(x_ref[...], g_ref[...], wg_ref, wu_ref, wd_ref)
    o_ref[...] = x_new
    x_prev[...] = x_new


def _ffn_proj_call(x, seq, ffn_consts, consts, tables):
    n = x.shape[0]
    tm = min(TM_ROWS, seq)
    n_tiles = n // tm
    n_seq_tiles = seq // tm
    ffn_tile = lambda i: jnp.minimum(i, n_tiles - 1)
    proj_tile = lambda i: jnp.maximum(i - 1, 0)
    proj_row_spec = lambda width: pl.BlockSpec((tm, width), lambda i: (proj_tile(i), 0))
    ffn_row_spec = pl.BlockSpec((tm, D_MODEL), lambda i: (ffn_tile(i), 0))
    pos_spec = pl.BlockSpec((tm, LANES), lambda i: (proj_tile(i) % n_seq_tiles, 0))
    pos_t_spec = pl.BlockSpec((HALF, tm), lambda i: (0, proj_tile(i) % n_seq_tiles))
    row_widths = {1: MLA_HEADS * MLA_QK_PAD, 3: SWA_HEADS * LANES, 4: SWA_KV_HEADS * LANES,
                  5: SWA_KV_HEADS * SWA_HEAD_DIM}
    col_heights = {0: MLA_HEADS * MLA_QK_PAD, 2: MLA_HEADS * MLA_V_ROWS}
    out_specs, out_shape = [ffn_row_spec], [jax.ShapeDtypeStruct(x.shape, F32)]
    for idx in range(6):
        if idx in row_widths:
            out_specs.append(proj_row_spec(row_widths[idx]))
            out_shape.append(jax.ShapeDtypeStruct((n, row_widths[idx]), BF16))
        else:
            out_specs.append(pl.BlockSpec((1, col_heights[idx], tm), lambda i: (proj_tile(i), 0, 0)))
            out_shape.append(jax.ShapeDtypeStruct((n_tiles, col_heights[idx], tm), BF16))
    assert len(ffn_consts) == N_FFN_CONSTS and len(consts) == N_PROJ_CONSTS and len(tables) == N_PROJ_TABLES
    return pl.pallas_call(
        _ffn_proj_kernel,
        grid=(n_tiles + 1,),
        in_specs=([ffn_row_spec] + [_const_spec(c.shape) for c in ffn_consts]
                  + [_const_spec(c.shape) for c in consts]
                  + [pos_spec, pos_spec, pos_t_spec, pos_t_spec]),
        out_specs=out_specs,
        out_shape=out_shape,
        scratch_shapes=[pltpu.VMEM((tm, D_MODEL), F32)],
        compiler_params=pltpu.CompilerParams(
            dimension_semantics=("arbitrary",), vmem_limit_bytes=VMEM_LIMIT),
        name="ffn_proj",
    )(x, *ffn_consts, *consts, *tables)


MLA_STREAMS = 4


def _mla_kernel(q_ref, k_ref, v_ref, o_ref, s_ref, p_ref, acc_ref):
    t = q_ref.shape[3]
    tk = t // 2
    qi = pl.program_id(2)
    heads = range(MLA_STREAMS)

    def scores(tile, half, h):
        start = pl.multiple_of(tile * t, t) + half * tk
        k = k_ref[0, pl.ds(start, tk), h * MLA_QK_PAD:(h + 1) * MLA_QK_PAD]
        return _dot(k, q_ref[0, 0, h * MLA_QK_PAD:(h + 1) * MLA_QK_PAD, :])

    def weighted_values(tile, half, h, buf):
        v = v_ref[0, tile, h * MLA_V_ROWS:(h + 1) * MLA_V_ROWS, half * tk:(half + 1) * tk]
        return _dot(v, p_ref[buf, h, :, :t])

    def phase(cur, state, nxt_tile, prev_tile, mask_half=None):
        new_state = []
        for h in heads:
            m, alpha_prev = state[h]
            if prev_tile is not None:
                pv = weighted_values(prev_tile, 1 - cur, h, 1 - cur)
            if nxt_tile is not None:
                s_ref[1 - cur, h, :, :t] = scores(nxt_tile, 1 - cur, h)
            s = s_ref[cur, h, :, :t]
            if mask_half is not None:
                key = lax.broadcasted_iota(jnp.int32, (tk, t), 0) + mask_half * tk
                qry = lax.broadcasted_iota(jnp.int32, (tk, t), 1)
                s = jnp.where(key <= qry, s, NEG_BIG)
            m_new = jnp.maximum(m, jnp.max(s, axis=0, keepdims=True))
            p = jnp.exp2(s - m_new)
            p_ref[cur, h, :, :t] = p.astype(BF16)
            alpha = jnp.exp2(m - m_new)
            if prev_tile is not None:
                acc_ref[h, :, :t] = alpha_prev * acc_ref[h, :, :t] + pv
            new_state.append((m_new, alpha))
        return tuple(new_state)

    state = tuple((jnp.full((1, t), NEG_BIG, F32), jnp.zeros((1, t), F32)) for _ in heads)
    for h in heads:
        acc_ref[h, :, :t] = jnp.zeros((MLA_V_ROWS, t), F32)
        s_ref[0, h, :, :t] = scores(qi, 0, h)
    state = phase(0, state, qi, None, mask_half=0)
    state = phase(1, state, 0, qi, mask_half=1)

    def body(jj, state):
        prev_tile = jnp.where(jj == 0, qi, jj - 1)
        state = phase(0, state, jj, prev_tile)
        return phase(1, state, jj + 1, jj)

    state = lax.fori_loop(0, qi, body, state)
    last_tile = jnp.maximum(qi - 1, 0)
    for h in heads:
        m, alpha_prev = state[h]
        acc = alpha_prev * acc_ref[h, :, :t] + weighted_values(last_tile, 1, h, 1)
        o_ref[0, :, h * MLA_V:(h + 1) * MLA_V] = (acc[:MLA_V] / acc[MLA_V:MLA_V + 1]).T


def _mla_call(qt, ka, vt):
    b, nq, _, t = qt.shape
    s = ka.shape[1]
    qw = MLA_STREAMS * MLA_QK_PAD
    vw = MLA_STREAMS * MLA_V_ROWS
    ow = MLA_STREAMS * MLA_V
    return pl.pallas_call(
        _mla_kernel,
        grid=(b, MLA_HEADS // MLA_STREAMS, nq),
        in_specs=[pl.BlockSpec((1, 1, qw, t), lambda bi, h, i: (bi, i, h, 0)),
                  pl.BlockSpec((1, s, qw), lambda bi, h, i: (bi, 0, h), pipeline_mode=pl.Buffered(1)),
                  pl.BlockSpec((1, nq, vw, t), lambda bi, h, i: (bi, 0, h, 0),
                               pipeline_mode=pl.Buffered(1))],
        out_specs=pl.BlockSpec((1, t, ow), lambda bi, h, i: (bi, i, h)),
        out_shape=jax.ShapeDtypeStruct((b, s, MLA_WIDTH), F32),
        scratch_shapes=[pltpu.VMEM((2, MLA_STREAMS, t // 2, t + LANES), F32),
                        pltpu.VMEM((2, MLA_STREAMS, t // 2, t + LANES), BF16),
                        pltpu.VMEM((MLA_STREAMS, MLA_V_ROWS, t + LANES), F32)],
        compiler_params=pltpu.CompilerParams(
            dimension_semantics=("arbitrary", "arbitrary", "arbitrary"),
            vmem_limit_bytes=VMEM_LIMIT),
        name="mla_attn",
    )(qt, ka, vt)


def _swa_kernel(sink_ref, q_ref, kc_ref, kp_ref, vc_ref, vp_ref, o_ref, kf_ref, vf_ref):
    blk = SWA_BLOCK
    n_blk = q_ref.shape[1] // blk
    tile = pl.program_id(1)
    kf_ref[0:blk, :] = kp_ref[0]
    kf_ref[blk:, :] = kc_ref[0]
    vf_ref[0:blk, :] = vp_ref[0]
    vf_ref[blk:, :] = vc_ref[0]

    rows = SWA_GROUP * blk
    q_rel = lax.broadcasted_iota(jnp.int32, (rows, 2 * blk), 0) & (blk - 1)
    k_rel = lax.broadcasted_iota(jnp.int32, (rows, 2 * blk), 1)
    in_window = (k_rel > q_rel) & (k_rel <= q_rel + blk)
    head_of_row = lax.broadcasted_iota(jnp.int32, (rows, 1), 0) // blk
    lane = lax.broadcasted_iota(jnp.int32, (1, LANES), 1)
    low = lane < SWA_HEAD_DIM

    for c in range(SWA_KV_HEADS):
        sink = jnp.zeros((rows, 1), F32)
        for g in range(SWA_GROUP):
            sink = jnp.where(head_of_row == g, sink_ref[c * SWA_GROUP + g], sink)
        for n in range(n_blk):
            q = jnp.concatenate(
                [q_ref[0, n * blk:(n + 1) * blk, (c * SWA_GROUP + g) * LANES:(c * SWA_GROUP + g + 1) * LANES]
                 for g in range(SWA_GROUP)], axis=0)
            k = kf_ref[n * blk:(n + 2) * blk, c * LANES:(c + 1) * LANES]
            v = vf_ref[n * blk:(n + 2) * blk, :]
            s = _dot_nt(q, k)
            valid = in_window
            if n == 0:
                valid = valid & ((k_rel >= blk) | (tile > 0))
            s = jnp.where(valid, s, NEG_BIG)
            m = jnp.maximum(jnp.max(s, axis=-1, keepdims=True), sink)
            e = jnp.exp(s - m)
            denom = jnp.sum(e, axis=-1, keepdims=True) + jnp.exp(sink - m)
            o = _dot(e.astype(BF16), v) / denom
            for j in range(SWA_GROUP // 2):
                a = o[(2 * j) * blk:(2 * j + 1) * blk]
                bb = o[(2 * j + 1) * blk:(2 * j + 2) * blk]
                if c == 0:
                    bb = pltpu.roll(bb, SWA_HEAD_DIM, 1)
                else:
                    a = pltpu.roll(a, SWA_HEAD_DIM, 1)
                col = (c * SWA_GROUP // 2 + j) * LANES
                o_ref[0, n * blk:(n + 1) * blk, col:col + LANES] = jnp.where(low, a, bb)


def _swa_call(qb, kb, vb, sinks):
    b, s, _ = qb.shape
    tq = min(SWA_TQ, s)
    ratio = tq // SWA_BLOCK
    kw = SWA_KV_HEADS * LANES
    vw = SWA_KV_HEADS * SWA_HEAD_DIM
    prev = lambda bi, i: (bi, jnp.maximum(i * ratio - 1, 0), 0)
    cur = lambda bi, i: (bi, i, 0)
    return pl.pallas_call(
        _swa_kernel,
        grid=(b, s // tq),
        in_specs=[pl.BlockSpec(memory_space=pltpu.SMEM),
                  pl.BlockSpec((1, tq, SWA_HEADS * LANES), cur),
                  pl.BlockSpec((1, tq, kw), cur),
                  pl.BlockSpec((1, SWA_BLOCK, kw), prev),
                  pl.BlockSpec((1, tq, vw), cur),
                  pl.BlockSpec((1, SWA_BLOCK, vw), prev)],
        out_specs=pl.BlockSpec((1, tq, SWA_WIDTH), cur),
        out_shape=jax.ShapeDtypeStruct((b, s, SWA_WIDTH), F32),
        scratch_shapes=[pltpu.VMEM((tq + SWA_BLOCK, kw), BF16),
                        pltpu.VMEM((tq + SWA_BLOCK, vw), BF16)],
        compiler_params=pltpu.CompilerParams(
            dimension_semantics=("arbitrary", "arbitrary"), vmem_limit_bytes=VMEM_LIMIT),
        name="swa_attn",
    )(sinks, qb, kb, kb, vb, vb)


def _out_ffn_kernel(x_ref, oa_ref, ob_ref, ga_ref, gb_ref, wo_ref, g_ref, wg_ref, wu_ref, wd_ref, o_ref):
    na = _rms(oa_ref[...], ga_ref[...], MLA_WIDTH).astype(BF16)
    nb = _rms(ob_ref[...], gb_ref[...], SWA_WIDTH).astype(BF16)
    y = _dot(na, wo_ref[0:MLA_WIDTH, :]) + _dot(nb, wo_ref[MLA_WIDTH:, :])
    o_ref[...] = _ffn_block(x_ref[...] + y, g_ref[...], wg_ref, wu_ref, wd_ref)


def _out_ffn_call(x, oa, ob, ga, gb, wo, ffn_consts):
    n = x.shape[0]
    tm = min(TM_ROWS, n)
    return pl.pallas_call(
        _out_ffn_kernel,
        grid=(n // tm,),
        in_specs=[_row_spec(tm, D_MODEL), _row_spec(tm, MLA_WIDTH), _row_spec(tm, SWA_WIDTH),
                  _const_spec(ga.shape), _const_spec(gb.shape), _const_spec(wo.shape)]
                 + [_const_spec(c.shape) for c in ffn_consts],
        out_specs=_row_spec(tm, D_MODEL),
        out_shape=jax.ShapeDtypeStruct(x.shape, F32),
        compiler_params=pltpu.CompilerParams(
            dimension_semantics=("arbitrary",), vmem_limit_bytes=VMEM_LIMIT),
        name="out_ffn",
    )(x, oa, ob, ga, gb, wo, *ffn_consts)


def _pair_cols(n_heads, head_dim, base=0):
    half = head_dim // 2
    cols = []
    for pair in range(n_heads // 2):
        for part in range(2):
            for sub in range(2):
                start = base + (2 * pair + sub) * head_dim + part * half
                cols.extend(range(start, start + half))
    return np.asarray(cols, np.int32)


def _dup_cols(n_heads, head_dim, base=0):
    half = head_dim // 2
    cols = []
    for hd in range(n_heads):
        for part in range(2):
            start = base + hd * head_dim + part * half
            cols.extend(list(range(start, start + half)) * 2)
    return np.asarray(cols, np.int32)


def _layer_consts(l, mix_norm, w_in, mla_q_a_norm, mla_w_q_b, mla_kv_a_norm, mla_w_kv_b,
                  mla_q_norm, mla_k_norm, swa_q_norm, swa_k_norm):
    o_kpe = MLA_Q_RANK + MLA_KV_RANK
    o_qs = o_kpe + MLA_ROPE
    o_ks = o_qs + SWA_WIDTH
    o_vs = o_ks + SWA_KV_HEADS * SWA_HEAD_DIM
    in_cols = np.concatenate([
        np.arange(0, o_kpe, dtype=np.int32),
        _dup_cols(1, MLA_ROPE, o_kpe),
        _pair_cols(SWA_HEADS, SWA_HEAD_DIM, o_qs),
        _dup_cols(SWA_KV_HEADS, SWA_HEAD_DIM, o_ks),
        np.arange(o_vs, o_vs + SWA_KV_HEADS * SWA_HEAD_DIM, dtype=np.int32)])
    k_cols = np.concatenate(
        [np.arange(h * (MLA_NOPE + MLA_V), h * (MLA_NOPE + MLA_V) + MLA_NOPE) for h in range(MLA_HEADS)])
    v_cols = np.concatenate(
        [np.arange(h * (MLA_NOPE + MLA_V) + MLA_NOPE, (h + 1) * (MLA_NOPE + MLA_V)) for h in range(MLA_HEADS)])
    rope_gain = _dup_cols(1, MLA_ROPE, MLA_NOPE)
    swa_gain = _dup_cols(1, SWA_HEAD_DIM)
    row = lambda v: v.reshape(1, -1).astype(F32)
    return (
        row(mix_norm[l]),
        w_in[l][:, in_cols].astype(BF16),
        row(mla_q_a_norm[l]),
        mla_w_q_b[l].T.astype(BF16),
        row(mla_kv_a_norm[l]),
        mla_w_kv_b[l][:, k_cols].astype(BF16),
        mla_w_kv_b[l][:, v_cols].T.astype(BF16),
        jnp.broadcast_to(mla_q_norm[l].astype(F32)[:, None], (MLA_QK, LANES)),
        row(mla_k_norm[l][:MLA_NOPE]),
        row(mla_k_norm[l][rope_gain]),
        row(swa_q_norm[l][swa_gain]),
        row(swa_k_norm[l][swa_gain]),
    )


def _rope_tables(seq):
    pos = jnp.arange(seq, dtype=F32)
    inv = 1.0 / (ROPE_THETA ** (jnp.arange(0, 2 * HALF, 2, dtype=F32) / (2 * HALF)))
    ang = pos[:, None] * inv[None, :]
    c, s = jnp.cos(ang), jnp.sin(ang)
    return (jnp.concatenate([c, c, c, c], axis=1), jnp.concatenate([-s, -s, s, s], axis=1), c.T, s.T)


def kernel(x, ffn1_norm, ffn1_w_gate, ffn1_w_up, ffn1_w_down, mix_norm, w_in, mla_q_a_norm, mla_w_q_b, mla_kv_a_norm, mla_w_kv_b, mla_q_norm, mla_k_norm, swa_q_norm, swa_k_norm, swa_sinks, mla_out_norm, swa_out_norm, w_o, ffn2_norm, ffn2_w_gate, ffn2_w_up, ffn2_w_down):
    b, s, d = x.shape
    depth = w_in.shape[0]
    tables = _rope_tables(s)
    row = lambda v: v.reshape(1, -1).astype(F32)
    xf = x.reshape(b * s, d)
    for l in range(depth):
        ffn1 = (row(ffn1_norm[l]), ffn1_w_gate[l].astype(BF16), ffn1_w_up[l].astype(BF16),
                ffn1_w_down[l].astype(BF16))
        ffn2 = (row(ffn2_norm[l]), ffn2_w_gate[l].astype(BF16), ffn2_w_up[l].astype(BF16),
                ffn2_w_down[l].astype(BF16))
        consts = _layer_consts(l, mix_norm, w_in, mla_q_a_norm, mla_w_q_b, mla_kv_a_norm,
                               mla_w_kv_b, mla_q_norm, mla_k_norm, swa_q_norm, swa_k_norm)
        xf, qt, ka, vt, qb, kb, vb = _ffn_proj_call(xf, s, ffn1, consts, tables)
        shp = lambda t: t.reshape(b, s, t.shape[-1])
        tiles = lambda t: t.reshape(b, -1, t.shape[-2], t.shape[-1])
        oa = _mla_call(tiles(qt), shp(ka), tiles(vt))
        ob = _swa_call(shp(qb), shp(kb), shp(vb), swa_sinks[l].astype(F32))
        xf = _out_ffn_call(xf, oa.reshape(b * s, MLA_WIDTH), ob.reshape(b * s, SWA_WIDTH),
                           row(mla_out_norm[l]), row(swa_out_norm[l]), w_o[l].astype(BF16), ffn2)
    return xf.reshape(b, s, d)
```

```python
import functools

import numpy as np
import jax
import jax.numpy as jnp
from jax import lax
from jax.experimental import pallas as pl
from jax.experimental.pallas import tpu as pltpu

D_MODEL = 1024
EPS = 1e-6
ROPE_THETA = 10000.0
MLA_HEADS = 4
MLA_Q_RANK = 256
MLA_KV_RANK = 128
MLA_NOPE = 128
MLA_ROPE = 64
MLA_V = 128
MLA_QK = MLA_NOPE + MLA_ROPE
MLA_WIDTH = MLA_HEADS * MLA_V
SWA_HEADS = 8
SWA_KV_HEADS = 2
SWA_GROUP = SWA_HEADS // SWA_KV_HEADS
SWA_HEAD_DIM = 64
SWA_BLOCK = 128
SWA_WIDTH = SWA_HEADS * SWA_HEAD_DIM
D_FF = 2816

LANES = 128
HALF = SWA_HEAD_DIM // 2
MLA_QK_PAD = 2 * LANES
BF16_ROWS = 16
MLA_V_ROWS = MLA_V + BF16_ROWS
NEG_BIG = -1e30

VMEM_LIMIT = 56 * 1024 * 1024

LOG2E = 1.4426950408889634

TM_ROWS = 512
SWA_TQ = 512

BF16 = jnp.bfloat16
F32 = jnp.float32


def _rms(t, gain, width):
    ss = jnp.sum(t * t, axis=-1, keepdims=True)
    return t * lax.rsqrt(ss * (1.0 / width) + EPS) * gain


def _dot(a, b):
    return jnp.dot(a, b, preferred_element_type=F32)


def _dot_nt(a, b):
    return lax.dot_general(a, b, (((1,), (1,)), ((), ())), preferred_element_type=F32)


def _ffn_block(x, g, wg_ref, wu_ref, wd_ref):
    h = _rms(x, g, D_MODEL).astype(BF16)
    gate = _dot(h, wg_ref[...])
    up = _dot(h, wu_ref[...])
    act = (gate * jax.nn.sigmoid(gate) * up).astype(BF16)
    return x + 0.5 * _dot(act, wd_ref[...])


def _const_spec(shape):
    nd = len(shape)
    return pl.BlockSpec(shape, lambda *_: (0,) * nd, pipeline_mode=pl.Buffered(1))


def _row_spec(tm, width):
    return pl.BlockSpec((tm, width), lambda i: (i, 0))


C_CQ = 0
C_CKV = C_CQ + MLA_Q_RANK
C_KPE = C_CKV + MLA_KV_RANK
C_KS = C_KPE + LANES
C_END = C_KS + SWA_KV_HEADS * LANES
R_VS = SWA_HEADS * SWA_HEAD_DIM
R_END = R_VS + SWA_KV_HEADS * SWA_HEAD_DIM
SWA_V_ROWS = SWA_HEAD_DIM + BF16_ROWS


def _rope(t, cos, sin_signed):
    return t * cos + pltpu.roll(t, 2 * HALF, 1) * sin_signed


def _proj_body(x, g_ref, win_ref, wft_ref, gqa_ref, wqbt_ref, gkva_ref, wkb_ref, wvt_ref,
               gq_ref, gkn_ref, gkr_ref, gsq_ref, gsk_ref, cos_ref, sin_ref, cost_ref, sint_ref,
               qt_ref, ka_ref, vt_ref, qbt_ref, kb_ref, vbt_ref):
    tm = x.shape[0]
    q_scale = MLA_QK ** -0.5 * LOG2E
    swa_scale = SWA_HEAD_DIM ** -0.5 * LOG2E
    cos = cos_ref[...]
    sin = sin_ref[...]

    h = _rms(x, g_ref[...], D_MODEL).astype(BF16)
    proj = _dot(h, win_ref[...])
    proj_t = _dot_nt(wft_ref[...], h)

    cq_t = _rms(proj[:, C_CQ:C_CKV], gqa_ref[...], MLA_Q_RANK).T.astype(BF16)
    q_up = _dot(wqbt_ref[...], cq_t)
    gq = jnp.tile(gq_ref[...], (1, tm // LANES))
    cos_t = cost_ref[...]
    sin_t = sint_ref[...]
    zeros = jnp.zeros((HALF, tm), BF16)
    for hd in range(MLA_HEADS):
        blk = q_up[hd * MLA_QK:(hd + 1) * MLA_QK]
        ss = jnp.sum(blk * blk, axis=0, keepdims=True)
        qn = blk * (lax.rsqrt(ss * (1.0 / MLA_QK) + EPS) * q_scale) * gq
        t1 = qn[MLA_NOPE:MLA_NOPE + HALF]
        t2 = qn[MLA_NOPE + HALF:]
        base = hd * MLA_QK_PAD
        qt_ref[0, base:base + MLA_NOPE, :] = qn[:MLA_NOPE].astype(BF16)
        qt_ref[0, base + MLA_NOPE:base + MLA_NOPE + HALF, :] = (t1 * cos_t - t2 * sin_t).astype(BF16)
        qt_ref[0, base + MLA_NOPE + HALF:base + MLA_NOPE + 2 * HALF, :] = zeros
        qt_ref[0, base + MLA_NOPE + 2 * HALF:base + MLA_NOPE + 3 * HALF, :] = (
            t2 * cos_t + t1 * sin_t).astype(BF16)
        qt_ref[0, base + MLA_NOPE + 3 * HALF:base + MLA_QK_PAD, :] = zeros

    ckv = _rms(proj[:, C_CKV:C_KPE], gkva_ref[...], MLA_KV_RANK)
    k_up = _dot(ckv.astype(BF16), wkb_ref[...])
    v_t = _dot(wvt_ref[...], ckv.T.astype(BF16)).astype(BF16)
    ones_row = (lax.broadcasted_iota(jnp.int32, (BF16_ROWS, tm), 0) == 0).astype(BF16)
    for hd in range(MLA_HEADS):
        vt_ref[0, hd * MLA_V_ROWS:hd * MLA_V_ROWS + MLA_V, :] = v_t[hd * MLA_V:(hd + 1) * MLA_V]
        vt_ref[0, hd * MLA_V_ROWS + MLA_V:(hd + 1) * MLA_V_ROWS, :] = ones_row
    kpe = proj[:, C_KPE:C_KS]
    ss_pe = 0.5 * jnp.sum(kpe * kpe, axis=-1, keepdims=True)
    kpe_rot = _rope(kpe * gkr_ref[...], cos, sin)
    for hd in range(MLA_HEADS):
        nope = k_up[:, hd * MLA_NOPE:(hd + 1) * MLA_NOPE]
        ss = jnp.sum(nope * nope, axis=-1, keepdims=True) + ss_pe
        r = lax.rsqrt(ss * (1.0 / MLA_QK) + EPS)
        ka_ref[:, hd * MLA_QK_PAD: hd * MLA_QK_PAD + MLA_NOPE] = (
            nope * r * gkn_ref[...]).astype(BF16)
        ka_ref[:, hd * MLA_QK_PAD + MLA_NOPE:(hd + 1) * MLA_QK_PAD] = (kpe_rot * r).astype(BF16)

    gsq = jnp.tile(gsq_ref[...], (1, tm // LANES))
    for hd in range(SWA_HEADS):
        blk = proj_t[hd * SWA_HEAD_DIM:(hd + 1) * SWA_HEAD_DIM]
        ss = jnp.sum(blk * blk, axis=0, keepdims=True)
        qn = blk * (lax.rsqrt(ss * (1.0 / SWA_HEAD_DIM) + EPS) * swa_scale) * gsq
        t1 = qn[:HALF]
        t2 = qn[HALF:]
        base = hd * LANES
        qbt_ref[0, base:base + HALF, :] = (t1 * cos_t - t2 * sin_t).astype(BF16)
        qbt_ref[0, base + HALF:base + 2 * HALF, :] = zeros
        qbt_ref[0, base + 2 * HALF:base + 3 * HALF, :] = (t2 * cos_t + t1 * sin_t).astype(BF16)
        qbt_ref[0, base + 3 * HALF:base + LANES, :] = zeros
    for c in range(SWA_KV_HEADS):
        vbt_ref[0, c * SWA_V_ROWS:c * SWA_V_ROWS + SWA_HEAD_DIM, :] = (
            proj_t[R_VS + c * SWA_HEAD_DIM:R_VS + (c + 1) * SWA_HEAD_DIM].astype(BF16))
        vbt_ref[0, c * SWA_V_ROWS + SWA_HEAD_DIM:(c + 1) * SWA_V_ROWS, :] = ones_row
    for c in range(SWA_KV_HEADS):
        xk = proj[:, C_KS + c * LANES: C_KS + (c + 1) * LANES]
        ss = 0.5 * jnp.sum(xk * xk, axis=-1, keepdims=True)
        r = lax.rsqrt(ss * (1.0 / SWA_HEAD_DIM) + EPS)
        kb_ref[:, c * LANES:(c + 1) * LANES] = _rope(xk * r * gsk_ref[...], cos, sin).astype(BF16)


N_FFN_CONSTS = 4
N_PROJ_CONSTS = 13
N_PROJ_TABLES = 4


def _ffn_proj_kernel(*refs):
    x_ref = refs[0]
    ffn_refs = refs[1:1 + N_FFN_CONSTS]
    proj_refs = refs[1 + N_FFN_CONSTS:1 + N_FFN_CONSTS + N_PROJ_CONSTS + N_PROJ_TABLES]
    o_ref = refs[1 + N_FFN_CONSTS + N_PROJ_CONSTS + N_PROJ_TABLES]
    proj_out_refs = refs[-7:-1]
    x_prev = refs[-1]

    @pl.when(pl.program_id(0) == 0)
    def _():
        x_prev[...] = jnp.zeros_like(x_prev)

    _proj_body(x_prev[...], *proj_refs, *proj_out_refs)
    g_ref, wg_ref, wu_ref, wd_ref = ffn_refs
    x_new = _ffn_block(x_ref[...], g_ref[...], wg_ref, wu_ref, wd_ref)
    o_ref[...] = x_new
    x_prev[...] = x_new


def _ffn_proj_call(x, seq, ffn_consts, consts, tables):
    n = x.shape[0]
    tm = min(TM_ROWS, seq)
    n_tiles = n // tm
    n_seq_tiles = seq // tm
    ffn_tile = lambda i: jnp.minimum(i, n_tiles - 1)
    proj_tile = lambda i: jnp.maximum(i - 1, 0)
    proj_row_spec = lambda width: pl.BlockSpec((tm, width), lambda i: (proj_tile(i), 0))
    ffn_row_spec = pl.BlockSpec((tm, D_MODEL), lambda i: (ffn_tile(i), 0))
    pos_spec = pl.BlockSpec((tm, LANES), lambda i: (proj_tile(i) % n_seq_tiles, 0))
    pos_t_spec = pl.BlockSpec((HALF, tm), lambda i: (0, proj_tile(i) % n_seq_tiles))
    row_widths = {1: MLA_HEADS * MLA_QK_PAD, 4: SWA_KV_HEADS * LANES}
    col_heights = {0: MLA_HEADS * MLA_QK_PAD, 2: MLA_HEADS * MLA_V_ROWS,
                   3: SWA_HEADS * LANES, 5: SWA_KV_HEADS * SWA_V_ROWS}
    out_specs, out_shape = [ffn_row_spec], [jax.ShapeDtypeStruct(x.shape, F32)]
    for idx in range(6):
        if idx in row_widths:
            out_specs.append(proj_row_spec(row_widths[idx]))
            out_shape.append(jax.ShapeDtypeStruct((n, row_widths[idx]), BF16))
        else:
            out_specs.append(pl.BlockSpec((1, col_heights[idx], tm), lambda i: (proj_tile(i), 0, 0)))
            out_shape.append(jax.ShapeDtypeStruct((n_tiles, col_heights[idx], tm), BF16))
    assert len(ffn_consts) == N_FFN_CONSTS and len(consts) == N_PROJ_CONSTS and len(tables) == N_PROJ_TABLES
    return pl.pallas_call(
        _ffn_proj_kernel,
        grid=(n_tiles + 1,),
        in_specs=([ffn_row_spec] + [_const_spec(c.shape) for c in ffn_consts]
                  + [_const_spec(c.shape) for c in consts]
                  + [pos_spec, pos_spec, pos_t_spec, pos_t_spec]),
        out_specs=out_specs,
        out_shape=out_shape,
        scratch_shapes=[pltpu.VMEM((tm, D_MODEL), F32)],
        compiler_params=pltpu.CompilerParams(
            dimension_semantics=("arbitrary",), vmem_limit_bytes=VMEM_LIMIT),
        name="ffn_proj",
    )(x, *ffn_consts, *consts, *tables)


MLA_STREAMS = 4


def _mla_kernel(q_ref, k_ref, v_ref, o_ref, s_ref, p_ref, acc_ref):
    t = q_ref.shape[3]
    tk = t // 2
    qi = pl.program_id(2)
    heads = range(MLA_STREAMS)

    def scores(tile, half, h):
        start = pl.multiple_of(tile * t, t) + half * tk
        k = k_ref[0, pl.ds(start, tk), h * MLA_QK_PAD:(h + 1) * MLA_QK_PAD]
        return _dot(k, q_ref[0, 0, h * MLA_QK_PAD:(h + 1) * MLA_QK_PAD, :])

    def weighted_values(tile, half, h, buf):
        v = v_ref[0, tile, h * MLA_V_ROWS:(h + 1) * MLA_V_ROWS, half * tk:(half + 1) * tk]
        return _dot(v, p_ref[buf, h, :, :t])

    def phase(cur, state, nxt_tile, prev_tile, mask_half=None):
        new_state = []
        for h in heads:
            m, alpha_prev = state[h]
            if prev_tile is not None:
                pv = weighted_values(prev_tile, 1 - cur, h, 1 - cur)
            if nxt_tile is not None:
                s_ref[1 - cur, h, :, :t] = scores(nxt_tile, 1 - cur, h)
            s = s_ref[cur, h, :, :t]
            if mask_half is not None:
                key = lax.broadcasted_iota(jnp.int32, (tk, t), 0) + mask_half * tk
                qry = lax.broadcasted_iota(jnp.int32, (tk, t), 1)
                s = jnp.where(key <= qry, s, NEG_BIG)
            m_new = jnp.maximum(m, jnp.max(s, axis=0, keepdims=True))
            p = jnp.exp2(s - m_new)
            p_ref[cur, h, :, :t] = p.astype(BF16)
            alpha = jnp.exp2(m - m_new)
            if prev_tile is not None:
                acc_ref[h, :, :t] = alpha_prev * acc_ref[h, :, :t] + pv
            new_state.append((m_new, alpha))
        return tuple(new_state)

    state = tuple((jnp.full((1, t), NEG_BIG, F32), jnp.zeros((1, t), F32)) for _ in heads)
    for h in heads:
        acc_ref[h, :, :t] = jnp.zeros((MLA_V_ROWS, t), F32)
        s_ref[0, h, :, :t] = scores(qi, 0, h)
    state = phase(0, state, qi, None, mask_half=0)
    state = phase(1, state, 0, qi, mask_half=1)

    def body(jj, state):
        prev_tile = jnp.where(jj == 0, qi, jj - 1)
        state = phase(0, state, jj, prev_tile)
        return phase(1, state, jj + 1, jj)

    state = lax.fori_loop(0, qi, body, state)
    last_tile = jnp.maximum(qi - 1, 0)
    for h in heads:
        m, alpha_prev = state[h]
        acc = alpha_prev * acc_ref[h, :, :t] + weighted_values(last_tile, 1, h, 1)
        o_ref[0, :, h * MLA_V:(h + 1) * MLA_V] = (acc[:MLA_V] / acc[MLA_V:MLA_V + 1]).T


def _mla_call(qt, ka, vt):
    b, nq, _, t = qt.shape
    s = ka.shape[1]
    qw = MLA_STREAMS * MLA_QK_PAD
    vw = MLA_STREAMS * MLA_V_ROWS
    ow = MLA_STREAMS * MLA_V
    return pl.pallas_call(
        _mla_kernel,
        grid=(b, MLA_HEADS // MLA_STREAMS, nq),
        in_specs=[pl.BlockSpec((1, 1, qw, t), lambda bi, h, i: (bi, i, h, 0)),
                  pl.BlockSpec((1, s, qw), lambda bi, h, i: (bi, 0, h), pipeline_mode=pl.Buffered(1)),
                  pl.BlockSpec((1, nq, vw, t), lambda bi, h, i: (bi, 0, h, 0),
                               pipeline_mode=pl.Buffered(1))],
        out_specs=pl.BlockSpec((1, t, ow), lambda bi, h, i: (bi, i, h)),
        out_shape=jax.ShapeDtypeStruct((b, s, MLA_WIDTH), F32),
        scratch_shapes=[pltpu.VMEM((2, MLA_STREAMS, t // 2, t + LANES), F32),
                        pltpu.VMEM((2, MLA_STREAMS, t // 2, t + LANES), BF16),
                        pltpu.VMEM((MLA_STREAMS, MLA_V_ROWS, t + LANES), F32)],
        compiler_params=pltpu.CompilerParams(
            dimension_semantics=("arbitrary", "arbitrary", "arbitrary"),
            vmem_limit_bytes=VMEM_LIMIT),
        name="mla_attn",
    )(qt, ka, vt)


def _swa_kernel(sink_ref, q_ref, kc_ref, kp_ref, vc_ref, vp_ref, o_ref, kf_ref):
    blk = SWA_BLOCK
    t = q_ref.shape[3]
    n_blk = t // blk
    tile = pl.program_id(1)
    kf_ref[0:blk, :] = kp_ref[0]
    kf_ref[blk:, :] = kc_ref[0]

    cols = SWA_GROUP * blk
    k_rel = lax.broadcasted_iota(jnp.int32, (2 * blk, cols), 0)
    q_rel = lax.broadcasted_iota(jnp.int32, (2 * blk, cols), 1) & (blk - 1)
    in_window = (k_rel > q_rel) & (k_rel <= q_rel + blk)
    head_of_col = lax.broadcasted_iota(jnp.int32, (1, cols), 1) // blk

    sinks = []
    for c in range(SWA_KV_HEADS):
        sink = jnp.zeros((1, cols), F32)
        for g in range(SWA_GROUP):
            sink = jnp.where(head_of_col == g, sink_ref[c * SWA_GROUP + g] * LOG2E, sink)
        sinks.append(sink)

    def band_scores(c, n):
        q_t = jnp.concatenate(
            [q_ref[0, 0, (c * SWA_GROUP + g) * LANES:(c * SWA_GROUP + g + 1) * LANES, n * blk:(n + 1) * blk]
             for g in range(SWA_GROUP)], axis=1)
        k = kf_ref[n * blk:(n + 2) * blk, c * LANES:(c + 1) * LANES]
        return _dot(k, q_t)

    def band_softmax(c, n, s):
        valid = in_window
        if n == 0:
            valid = valid & ((k_rel >= blk) | (tile > 0))
        s = jnp.where(valid, s, NEG_BIG)
        m = jnp.maximum(jnp.max(s, axis=0, keepdims=True), sinks[c])
        return jnp.exp2(s - m).astype(BF16), m

    def band_output(c, n, p, m):
        v_rows = slice(c * SWA_V_ROWS, (c + 1) * SWA_V_ROWS)
        if n == 0:
            v_t = jnp.concatenate([vp_ref[0, 0, v_rows, :], vc_ref[0, 0, v_rows, 0:blk]], axis=1)
        else:
            v_t = vc_ref[0, 0, v_rows, (n - 1) * blk:(n + 1) * blk]
        o_t = _dot(v_t, p)
        denom = o_t[SWA_HEAD_DIM:SWA_HEAD_DIM + 1] + jnp.exp2(sinks[c] - m)
        o = o_t[:SWA_HEAD_DIM] / denom
        for j in range(SWA_GROUP // 2):
            pair_t = jnp.concatenate([o[:, (2 * j) * blk:(2 * j + 1) * blk],
                                      o[:, (2 * j + 1) * blk:(2 * j + 2) * blk]], axis=0)
            col = (c * SWA_GROUP // 2 + j) * LANES
            o_ref[0, n * blk:(n + 1) * blk, col:col + LANES] = pair_t.T

    bands = [(c, n) for c in range(SWA_KV_HEADS) for n in range(n_blk)]
    score_lead, softmax_lead = 3, 1
    s_vals, p_vals = {}, {}
    for step in range(len(bands) + score_lead):
        done = step - score_lead
        if done >= 0:
            band_output(*bands[done], *p_vals.pop(done))
        if step < len(bands):
            s_vals[step] = band_scores(*bands[step])
        ready = step - (score_lead - softmax_lead)
        if 0 <= ready < len(bands):
            p_vals[ready] = band_softmax(*bands[ready], s_vals.pop(ready))


def _swa_call(qbt, kb, vbt, sinks):
    b, n_tiles, qh, t = qbt.shape
    s = kb.shape[1]
    ratio = t // SWA_BLOCK
    kw = SWA_KV_HEADS * LANES
    vh = SWA_KV_HEADS * SWA_V_ROWS
    prev_tile = lambda i: jnp.maximum(i - 1, 0)
    return pl.pallas_call(
        _swa_kernel,
        grid=(b, n_tiles),
        in_specs=[pl.BlockSpec(memory_space=pltpu.SMEM),
                  pl.BlockSpec((1, 1, qh, t), lambda bi, i: (bi, i, 0, 0)),
                  pl.BlockSpec((1, t, kw), lambda bi, i: (bi, i, 0)),
                  pl.BlockSpec((1, SWA_BLOCK, kw), lambda bi, i: (bi, jnp.maximum(i * ratio - 1, 0), 0)),
                  pl.BlockSpec((1, 1, vh, t), lambda bi, i: (bi, i, 0, 0)),
                  pl.BlockSpec((1, 1, vh, SWA_BLOCK), lambda bi, i: (bi, prev_tile(i), 0, ratio - 1))],
        out_specs=pl.BlockSpec((1, t, SWA_WIDTH), lambda bi, i: (bi, i, 0)),
        out_shape=jax.ShapeDtypeStruct((b, s, SWA_WIDTH), F32),
        scratch_shapes=[pltpu.VMEM((t + SWA_BLOCK, kw), BF16)],
        compiler_params=pltpu.CompilerParams(
            dimension_semantics=("arbitrary", "arbitrary"), vmem_limit_bytes=VMEM_LIMIT),
        name="swa_attn",
    )(sinks, qbt, kb, kb, vbt, vbt)


def _out_ffn_kernel(x_ref, oa_ref, ob_ref, ga_ref, gb_ref, wo_ref, g_ref, wg_ref, wu_ref, wd_ref, o_ref):
    na = _rms(oa_ref[...], ga_ref[...], MLA_WIDTH).astype(BF16)
    nb = _rms(ob_ref[...], gb_ref[...], SWA_WIDTH).astype(BF16)
    y = _dot(na, wo_ref[0:MLA_WIDTH, :]) + _dot(nb, wo_ref[MLA_WIDTH:, :])
    o_ref[...] = _ffn_block(x_ref[...] + y, g_ref[...], wg_ref, wu_ref, wd_ref)


def _out_ffn_call(x, oa, ob, ga, gb, wo, ffn_consts):
    n = x.shape[0]
    tm = min(TM_ROWS, n)
    return pl.pallas_call(
        _out_ffn_kernel,
        grid=(n // tm,),
        in_specs=[_row_spec(tm, D_MODEL), _row_spec(tm, MLA_WIDTH), _row_spec(tm, SWA_WIDTH),
                  _const_spec(ga.shape), _const_spec(gb.shape), _const_spec(wo.shape)]
                 + [_const_spec(c.shape) for c in ffn_consts],
        out_specs=_row_spec(tm, D_MODEL),
        out_shape=jax.ShapeDtypeStruct(x.shape, F32),
        compiler_params=pltpu.CompilerParams(
            dimension_semantics=("arbitrary",), vmem_limit_bytes=VMEM_LIMIT),
        name="out_ffn",
    )(x, oa, ob, ga, gb, wo, *ffn_consts)


def _pair_cols(n_heads, head_dim, base=0):
    half = head_dim // 2
    cols = []
    for pair in range(n_heads // 2):
        for part in range(2):
            for sub in range(2):
                start = base + (2 * pair + sub) * head_dim + part * half
                cols.extend(range(start, start + half))
    return np.asarray(cols, np.int32)


def _dup_cols(n_heads, head_dim, base=0):
    half = head_dim // 2
    cols = []
    for hd in range(n_heads):
        for part in range(2):
            start = base + hd * head_dim + part * half
            cols.extend(list(range(start, start + half)) * 2)
    return np.asarray(cols, np.int32)


def _layer_consts(l, mix_norm, w_in, mla_q_a_norm, mla_w_q_b, mla_kv_a_norm, mla_w_kv_b,
                  mla_q_norm, mla_k_norm, swa_q_norm, swa_k_norm):
    o_kpe = MLA_Q_RANK + MLA_KV_RANK
    o_qs = o_kpe + MLA_ROPE
    o_ks = o_qs + SWA_WIDTH
    o_vs = o_ks + SWA_KV_HEADS * SWA_HEAD_DIM
    in_cols = np.concatenate([
        np.arange(0, o_kpe, dtype=np.int32),
        _dup_cols(1, MLA_ROPE, o_kpe),
        _dup_cols(SWA_KV_HEADS, SWA_HEAD_DIM, o_ks)])
    ft_cols = np.concatenate([
        np.arange(o_qs, o_ks, dtype=np.int32),
        np.arange(o_vs, o_vs + SWA_KV_HEADS * SWA_HEAD_DIM, dtype=np.int32)])
    k_cols = np.concatenate(
        [np.arange(h * (MLA_NOPE + MLA_V), h * (MLA_NOPE + MLA_V) + MLA_NOPE) for h in range(MLA_HEADS)])
    v_cols = np.concatenate(
        [np.arange(h * (MLA_NOPE + MLA_V) + MLA_NOPE, (h + 1) * (MLA_NOPE + MLA_V)) for h in range(MLA_HEADS)])
    rope_gain = _dup_cols(1, MLA_ROPE, MLA_NOPE)
    swa_gain = _dup_cols(1, SWA_HEAD_DIM)
    row = lambda v: v.reshape(1, -1).astype(F32)
    return (
        row(mix_norm[l]),
        w_in[l][:, in_cols].astype(BF16),
        w_in[l][:, ft_cols].T.astype(BF16),
        row(mla_q_a_norm[l]),
        mla_w_q_b[l].T.astype(BF16),
        row(mla_kv_a_norm[l]),
        mla_w_kv_b[l][:, k_cols].astype(BF16),
        mla_w_kv_b[l][:, v_cols].T.astype(BF16),
        jnp.broadcast_to(mla_q_norm[l].astype(F32)[:, None], (MLA_QK, LANES)),
        row(mla_k_norm[l][:MLA_NOPE]),
        row(mla_k_norm[l][rope_gain]),
        jnp.broadcast_to(swa_q_norm[l].astype(F32)[:, None], (SWA_HEAD_DIM, LANES)),
        row(swa_k_norm[l][swa_gain]),
    )


def _rope_tables(seq):
    pos = jnp.arange(seq, dtype=F32)
    inv = 1.0 / (ROPE_THETA ** (jnp.arange(0, 2 * HALF, 2, dtype=F32) / (2 * HALF)))
    ang = pos[:, None] * inv[None, :]
    c, s = jnp.cos(ang), jnp.sin(ang)
    return (jnp.concatenate([c, c, c, c], axis=1), jnp.concatenate([-s, -s, s, s], axis=1), c.T, s.T)


def kernel(x, ffn1_norm, ffn1_w_gate, ffn1_w_up, ffn1_w_down, mix_norm, w_in, mla_q_a_norm, mla_w_q_b, mla_kv_a_norm, mla_w_kv_b, mla_q_norm, mla_k_norm, swa_q_norm, swa_k_norm, swa_sinks, mla_out_norm, swa_out_norm, w_o, ffn2_norm, ffn2_w_gate, ffn2_w_up, ffn2_w_down):
    b, s, d = x.shape
    depth = w_in.shape[0]
    tables = _rope_tables(s)
    row = lambda v: v.reshape(1, -1).astype(F32)
    xf = x.reshape(b * s, d)
    for l in range(depth):
        ffn1 = (row(ffn1_norm[l]), ffn1_w_gate[l].astype(BF16), ffn1_w_up[l].astype(BF16),
                ffn1_w_down[l].astype(BF16))
        ffn2 = (row(ffn2_norm[l]), ffn2_w_gate[l].astype(BF16), ffn2_w_up[l].astype(BF16),
                ffn2_w_down[l].astype(BF16))
        consts = _layer_consts(l, mix_norm, w_in, mla_q_a_norm, mla_w_q_b, mla_kv_a_norm,
                               mla_w_kv_b, mla_q_norm, mla_k_norm, swa_q_norm, swa_k_norm)
        xf, qt, ka, vt, qbt, kb, vbt = _ffn_proj_call(xf, s, ffn1, consts, tables)
        shp = lambda t: t.reshape(b, s, t.shape[-1])
        tiles = lambda t: t.reshape(b, -1, t.shape[-2], t.shape[-1])
        oa = _mla_call(tiles(qt), shp(ka), tiles(vt))
        ob = _swa_call(tiles(qbt), shp(kb), tiles(vbt), swa_sinks[l].astype(F32))
        xf = _out_ffn_call(xf, oa.reshape(b * s, MLA_WIDTH), ob.reshape(b * s, SWA_WIDTH),
                           row(mla_out_norm[l]), row(swa_out_norm[l]), w_o[l].astype(BF16), ffn2)
    return xf.reshape(b, s, d)
```

```python
import functools

import numpy as np
import jax
import jax.numpy as jnp
from jax import lax
from jax.experimental import pallas as pl
from jax.experimental.pallas import tpu as pltpu

D_MODEL = 1024
EPS = 1e-6
ROPE_THETA = 10000.0
MLA_HEADS = 4
MLA_Q_RANK = 256
MLA_KV_RANK = 128
MLA_NOPE = 128
MLA_ROPE = 64
MLA_V = 128
MLA_QK = MLA_NOPE + MLA_ROPE
MLA_WIDTH = MLA_HEADS * MLA_V
SWA_HEADS = 8
SWA_KV_HEADS = 2
SWA_GROUP = SWA_HEADS // SWA_KV_HEADS
SWA_HEAD_DIM = 64
SWA_BLOCK = 128
SWA_WIDTH = SWA_HEADS * SWA_HEAD_DIM
D_FF = 2816

LANES = 128
HALF = SWA_HEAD_DIM // 2
MLA_QK_PAD = 2 * LANES
BF16_ROWS = 16
MLA_V_ROWS = MLA_V + BF16_ROWS
NEG_BIG = -1e30

VMEM_LIMIT = 56 * 1024 * 1024

LOG2E = 1.4426950408889634

TM_ROWS = 512
SWA_TQ = 512

BF16 = jnp.bfloat16
F32 = jnp.float32


def _rms(t, gain, width):
    ss = jnp.sum(t * t, axis=-1, keepdims=True)
    return t * lax.rsqrt(ss * (1.0 / width) + EPS) * gain


def _dot(a, b):
    return jnp.dot(a, b, preferred_element_type=F32)


def _dot_nt(a, b):
    return lax.dot_general(a, b, (((1,), (1,)), ((), ())), preferred_element_type=F32)


def _ffn_stages(x, g, wg_ref, wu_ref, wd_ref, out):
    h = _rms(x, g, D_MODEL).astype(BF16)
    gate = _dot(h, wg_ref[0])
    up = _dot(h, wu_ref[0])
    yield
    act = (gate * jax.nn.sigmoid(gate) * up).astype(BF16)
    out.append(x + 0.5 * _dot(act, wd_ref[0]))


def _interleave(*staged):
    active = list(staged)
    while active:
        for gen in list(active):
            try:
                next(gen)
            except StopIteration:
                active.remove(gen)


def _layer_spec(stacked_shape, layer):
    nd = len(stacked_shape)
    return pl.BlockSpec((1,) + tuple(stacked_shape[1:]), lambda *_: (layer,) + (0,) * (nd - 1),
                        pipeline_mode=pl.Buffered(1))


def _row_spec(tm, width):
    return pl.BlockSpec((tm, width), lambda i: (i, 0))


C_CQ = 0
C_CKV = C_CQ + MLA_Q_RANK
C_KPE = C_CKV + MLA_KV_RANK
C_KS = C_KPE + LANES
C_END = C_KS + SWA_KV_HEADS * LANES
R_VS = SWA_HEADS * SWA_HEAD_DIM
R_END = R_VS + SWA_KV_HEADS * SWA_HEAD_DIM
SWA_V_ROWS = SWA_HEAD_DIM + BF16_ROWS


def _rope(t, cos, sin_signed):
    return t * cos + pltpu.roll(t, 2 * HALF, 1) * sin_signed


def _proj_stages(x, g_ref, win_ref, wft_ref, gqa_ref, wqbt_ref, gkva_ref, wkb_ref, wvt_ref,
                 gq_ref, gkn_ref, gkr_ref, gsq_ref, gsk_ref, cos_ref, sin_ref, cost_ref, sint_ref,
                 qt_ref, ka_ref, vt_ref, qbt_ref, kb_ref, vbt_ref):
    tm = x.shape[0]
    q_scale = MLA_QK ** -0.5 * LOG2E
    swa_scale = SWA_HEAD_DIM ** -0.5 * LOG2E

    h = _rms(x, g_ref[0], D_MODEL).astype(BF16)
    proj = _dot(h, win_ref[0])
    proj_t = _dot_nt(wft_ref[0], h)
    yield

    cq_t = _rms(proj[:, C_CQ:C_CKV], gqa_ref[0], MLA_Q_RANK).T.astype(BF16)
    ckv = _rms(proj[:, C_CKV:C_KPE], gkva_ref[0], MLA_KV_RANK)
    q_up = _dot(wqbt_ref[0], cq_t)
    k_up = _dot(ckv.astype(BF16), wkb_ref[0])
    v_t = _dot(wvt_ref[0], ckv.T.astype(BF16)).astype(BF16)
    yield

    cos = cos_ref[...]
    sin = sin_ref[...]
    cos_t = cost_ref[...]
    sin_t = sint_ref[...]
    zeros = jnp.zeros((HALF, tm), BF16)
    ones_row = (lax.broadcasted_iota(jnp.int32, (BF16_ROWS, tm), 0) == 0).astype(BF16)

    gq = jnp.tile(gq_ref[0], (1, tm // LANES))
    for hd in range(MLA_HEADS):
        blk = q_up[hd * MLA_QK:(hd + 1) * MLA_QK]
        ss = jnp.sum(blk * blk, axis=0, keepdims=True)
        qn = blk * (lax.rsqrt(ss * (1.0 / MLA_QK) + EPS) * q_scale) * gq
        t1 = qn[MLA_NOPE:MLA_NOPE + HALF]
        t2 = qn[MLA_NOPE + HALF:]
        base = hd * MLA_QK_PAD
        qt_ref[0, base:base + MLA_NOPE, :] = qn[:MLA_NOPE].astype(BF16)
        qt_ref[0, base + MLA_NOPE:base + MLA_NOPE + HALF, :] = (t1 * cos_t - t2 * sin_t).astype(BF16)
        qt_ref[0, base + MLA_NOPE + HALF:base + MLA_NOPE + 2 * HALF, :] = zeros
        qt_ref[0, base + MLA_NOPE + 2 * HALF:base + MLA_NOPE + 3 * HALF, :] = (
            t2 * cos_t + t1 * sin_t).astype(BF16)
        qt_ref[0, base + MLA_NOPE + 3 * HALF:base + MLA_QK_PAD, :] = zeros

    for hd in range(MLA_HEADS):
        vt_ref[0, hd * MLA_V_ROWS:hd * MLA_V_ROWS + MLA_V, :] = v_t[hd * MLA_V:(hd + 1) * MLA_V]
        vt_ref[0, hd * MLA_V_ROWS + MLA_V:(hd + 1) * MLA_V_ROWS, :] = ones_row
    kpe = proj[:, C_KPE:C_KS]
    ss_pe = 0.5 * jnp.sum(kpe * kpe, axis=-1, keepdims=True)
    kpe_rot = _rope(kpe * gkr_ref[0], cos, sin)
    for hd in range(MLA_HEADS):
        nope = k_up[:, hd * MLA_NOPE:(hd + 1) * MLA_NOPE]
        ss = jnp.sum(nope * nope, axis=-1, keepdims=True) + ss_pe
        r = lax.rsqrt(ss * (1.0 / MLA_QK) + EPS)
        ka_ref[:, hd * MLA_QK_PAD: hd * MLA_QK_PAD + MLA_NOPE] = (
            nope * r * gkn_ref[0]).astype(BF16)
        ka_ref[:, hd * MLA_QK_PAD + MLA_NOPE:(hd + 1) * MLA_QK_PAD] = (kpe_rot * r).astype(BF16)

    gsq = jnp.tile(gsq_ref[0], (1, tm // LANES))
    for hd in range(SWA_HEADS):
        blk = proj_t[hd * SWA_HEAD_DIM:(hd + 1) * SWA_HEAD_DIM]
        ss = jnp.sum(blk * blk, axis=0, keepdims=True)
        qn = blk * (lax.rsqrt(ss * (1.0 / SWA_HEAD_DIM) + EPS) * swa_scale) * gsq
        t1 = qn[:HALF]
        t2 = qn[HALF:]
        base = hd * LANES
        qbt_ref[0, base:base + HALF, :] = (t1 * cos_t - t2 * sin_t).astype(BF16)
        qbt_ref[0, base + HALF:base + 2 * HALF, :] = zeros
        qbt_ref[0, base + 2 * HALF:base + 3 * HALF, :] = (t2 * cos_t + t1 * sin_t).astype(BF16)
        qbt_ref[0, base + 3 * HALF:base + LANES, :] = zeros
    for c in range(SWA_KV_HEADS):
        vbt_ref[0, c * SWA_V_ROWS:c * SWA_V_ROWS + SWA_HEAD_DIM, :] = (
            proj_t[R_VS + c * SWA_HEAD_DIM:R_VS + (c + 1) * SWA_HEAD_DIM].astype(BF16))
        vbt_ref[0, c * SWA_V_ROWS + SWA_HEAD_DIM:(c + 1) * SWA_V_ROWS, :] = ones_row
    for c in range(SWA_KV_HEADS):
        xk = proj[:, C_KS + c * LANES: C_KS + (c + 1) * LANES]
        ss = 0.5 * jnp.sum(xk * xk, axis=-1, keepdims=True)
        r = lax.rsqrt(ss * (1.0 / SWA_HEAD_DIM) + EPS)
        kb_ref[:, c * LANES:(c + 1) * LANES] = _rope(xk * r * gsk_ref[0], cos, sin).astype(BF16)


N_FFN_CONSTS = 4
N_PROJ_CONSTS = 13
N_PROJ_TABLES = 4


def _ffn_proj_kernel(*refs):
    x_ref = refs[0]
    ffn_refs = refs[1:1 + N_FFN_CONSTS]
    proj_refs = refs[1 + N_FFN_CONSTS:1 + N_FFN_CONSTS + N_PROJ_CONSTS + N_PROJ_TABLES]
    o_ref = refs[1 + N_FFN_CONSTS + N_PROJ_CONSTS + N_PROJ_TABLES]
    proj_out_refs = refs[-7:-1]
    x_prev = refs[-1]

    @pl.when(pl.program_id(0) == 0)
    def _():
        x_prev[...] = jnp.zeros_like(x_prev)

    g_ref, wg_ref, wu_ref, wd_ref = ffn_refs
    x_new = []
    _interleave(_proj_stages(x_prev[...], *proj_refs, *proj_out_refs),
                _ffn_stages(x_ref[...], g_ref[0], wg_ref, wu_ref, wd_ref, x_new))
    o_ref[...] = x_new[0]
    x_prev[...] = x_new[0]


def _ffn_proj_call(x, seq, layer, ffn_consts, consts, tables):
    n = x.shape[0]
    tm = min(TM_ROWS, seq)
    n_tiles = n // tm
    n_seq_tiles = seq // tm
    ffn_tile = lambda i: jnp.minimum(i, n_tiles - 1)
    proj_tile = lambda i: jnp.maximum(i - 1, 0)
    proj_row_spec = lambda width: pl.BlockSpec((tm, width), lambda i: (proj_tile(i), 0))
    ffn_row_spec = pl.BlockSpec((tm, D_MODEL), lambda i: (ffn_tile(i), 0))
    pos_spec = pl.BlockSpec((tm, LANES), lambda i: (proj_tile(i) % n_seq_tiles, 0))
    pos_t_spec = pl.BlockSpec((HALF, tm), lambda i: (0, proj_tile(i) % n_seq_tiles))
    row_widths = {1: MLA_HEADS * MLA_QK_PAD, 4: SWA_KV_HEADS * LANES}
    col_heights = {0: MLA_HEADS * MLA_QK_PAD, 2: MLA_HEADS * MLA_V_ROWS,
                   3: SWA_HEADS * LANES, 5: SWA_KV_HEADS * SWA_V_ROWS}
    out_specs, out_shape = [ffn_row_spec], [jax.ShapeDtypeStruct(x.shape, F32)]
    for idx in range(6):
        if idx in row_widths:
            out_specs.append(proj_row_spec(row_widths[idx]))
            out_shape.append(jax.ShapeDtypeStruct((n, row_widths[idx]), BF16))
        else:
            out_specs.append(pl.BlockSpec((1, col_heights[idx], tm), lambda i: (proj_tile(i), 0, 0)))
            out_shape.append(jax.ShapeDtypeStruct((n_tiles, col_heights[idx], tm), BF16))
    assert len(ffn_consts) == N_FFN_CONSTS and len(consts) == N_PROJ_CONSTS and len(tables) == N_PROJ_TABLES
    return pl.pallas_call(
        _ffn_proj_kernel,
        grid=(n_tiles + 1,),
        in_specs=([ffn_row_spec] + [_layer_spec(c.shape, layer) for c in ffn_consts]
                  + [_layer_spec(c.shape, layer) for c in consts]
                  + [pos_spec, pos_spec, pos_t_spec, pos_t_spec]),
        out_specs=out_specs,
        out_shape=out_shape,
        scratch_shapes=[pltpu.VMEM((tm, D_MODEL), F32)],
        compiler_params=pltpu.CompilerParams(
            dimension_semantics=("arbitrary",), vmem_limit_bytes=VMEM_LIMIT),
        name="ffn_proj",
    )(x, *ffn_consts, *consts, *tables)


MLA_STREAMS = 4


def _mla_kernel(q_ref, k_ref, v_ref, o_ref, s_ref, p_ref, acc_ref):
    t = q_ref.shape[3]
    tk = t // 2
    qi = pl.program_id(2)
    heads = range(MLA_STREAMS)

    def scores(tile, half, h):
        start = pl.multiple_of(tile * t, t) + half * tk
        k = k_ref[0, pl.ds(start, tk), h * MLA_QK_PAD:(h + 1) * MLA_QK_PAD]
        return _dot(k, q_ref[0, 0, h * MLA_QK_PAD:(h + 1) * MLA_QK_PAD, :])

    def weighted_values(tile, half, h, buf):
        v = v_ref[0, tile, h * MLA_V_ROWS:(h + 1) * MLA_V_ROWS, half * tk:(half + 1) * tk]
        return _dot(v, p_ref[buf, h, :, :t])

    def phase(cur, state, nxt_tile, prev_tile, mask_half=None):
        new_state = []
        for h in heads:
            m, alpha_prev = state[h]
            if prev_tile is not None:
                pv = weighted_values(prev_tile, 1 - cur, h, 1 - cur)
            if nxt_tile is not None:
                s_ref[1 - cur, h, :, :t] = scores(nxt_tile, 1 - cur, h)
            s = s_ref[cur, h, :, :t]
            if mask_half is not None:
                key = lax.broadcasted_iota(jnp.int32, (tk, t), 0) + mask_half * tk
                qry = lax.broadcasted_iota(jnp.int32, (tk, t), 1)
                s = jnp.where(key <= qry, s, NEG_BIG)
            m_new = jnp.maximum(m, jnp.max(s, axis=0, keepdims=True))
            p = jnp.exp2(s - m_new)
            p_ref[cur, h, :, :t] = p.astype(BF16)
            alpha = jnp.exp2(m - m_new)
            if prev_tile is not None:
                acc_ref[h, :, :t] = alpha_prev * acc_ref[h, :, :t] + pv
            new_state.append((m_new, alpha))
        return tuple(new_state)

    state = tuple((jnp.full((1, t), NEG_BIG, F32), jnp.zeros((1, t), F32)) for _ in heads)
    for h in heads:
        acc_ref[h, :, :t] = jnp.zeros((MLA_V_ROWS, t), F32)
        s_ref[0, h, :, :t] = scores(qi, 0, h)
    state = phase(0, state, qi, None, mask_half=0)
    state = phase(1, state, 0, qi, mask_half=1)

    def body(jj, state):
        prev_tile = jnp.where(jj == 0, qi, jj - 1)
        state = phase(0, state, jj, prev_tile)
        return phase(1, state, jj + 1, jj)

    state = lax.fori_loop(0, qi, body, state)
    last_tile = jnp.maximum(qi - 1, 0)
    for h in heads:
        m, alpha_prev = state[h]
        acc = alpha_prev * acc_ref[h, :, :t] + weighted_values(last_tile, 1, h, 1)
        o_ref[0, :, h * MLA_V:(h + 1) * MLA_V] = (acc[:MLA_V] / acc[MLA_V:MLA_V + 1]).T


def _mla_call(qt, ka, vt):
    b, nq, _, t = qt.shape
    s = ka.shape[1]
    qw = MLA_STREAMS * MLA_QK_PAD
    vw = MLA_STREAMS * MLA_V_ROWS
    ow = MLA_STREAMS * MLA_V
    return pl.pallas_call(
        _mla_kernel,
        grid=(b, MLA_HEADS // MLA_STREAMS, nq),
        in_specs=[pl.BlockSpec((1, 1, qw, t), lambda bi, h, i: (bi, i, h, 0)),
                  pl.BlockSpec((1, s, qw), lambda bi, h, i: (bi, 0, h), pipeline_mode=pl.Buffered(1)),
                  pl.BlockSpec((1, nq, vw, t), lambda bi, h, i: (bi, 0, h, 0),
                               pipeline_mode=pl.Buffered(1))],
        out_specs=pl.BlockSpec((1, t, ow), lambda bi, h, i: (bi, i, h)),
        out_shape=jax.ShapeDtypeStruct((b, s, MLA_WIDTH), F32),
        scratch_shapes=[pltpu.VMEM((2, MLA_STREAMS, t // 2, t + LANES), F32),
                        pltpu.VMEM((2, MLA_STREAMS, t // 2, t + LANES), BF16),
                        pltpu.VMEM((MLA_STREAMS, MLA_V_ROWS, t + LANES), F32)],
        compiler_params=pltpu.CompilerParams(
            dimension_semantics=("arbitrary", "arbitrary", "arbitrary"),
            vmem_limit_bytes=VMEM_LIMIT),
        name="mla_attn",
    )(qt, ka, vt)


def _swa_kernel(sink_ref, q_ref, kc_ref, kp_ref, vc_ref, vp_ref, o_ref, kf_ref):
    blk = SWA_BLOCK
    t = q_ref.shape[3]
    n_blk = t // blk
    tile = pl.program_id(1)
    kf_ref[0:blk, :] = kp_ref[0]
    kf_ref[blk:, :] = kc_ref[0]

    cols = SWA_GROUP * blk
    k_rel = lax.broadcasted_iota(jnp.int32, (2 * blk, cols), 0)
    q_rel = lax.broadcasted_iota(jnp.int32, (2 * blk, cols), 1) & (blk - 1)
    in_window = (k_rel > q_rel) & (k_rel <= q_rel + blk)
    head_of_col = lax.broadcasted_iota(jnp.int32, (1, cols), 1) // blk

    sinks = []
    for c in range(SWA_KV_HEADS):
        sink = jnp.zeros((1, cols), F32)
        for g in range(SWA_GROUP):
            sink = jnp.where(head_of_col == g, sink_ref[c * SWA_GROUP + g] * LOG2E, sink)
        sinks.append(sink)

    def band_scores(c, n):
        q_t = jnp.concatenate(
            [q_ref[0, 0, (c * SWA_GROUP + g) * LANES:(c * SWA_GROUP + g + 1) * LANES, n * blk:(n + 1) * blk]
             for g in range(SWA_GROUP)], axis=1)
        k = kf_ref[n * blk:(n + 2) * blk, c * LANES:(c + 1) * LANES]
        return _dot(k, q_t)

    def band_softmax(c, n, s):
        valid = in_window
        if n == 0:
            valid = valid & ((k_rel >= blk) | (tile > 0))
        s = jnp.where(valid, s, NEG_BIG)
        m = jnp.maximum(jnp.max(s, axis=0, keepdims=True), sinks[c])
        return jnp.exp2(s - m).astype(BF16), m

    def band_output(c, n, p, m):
        v_rows = slice(c * SWA_V_ROWS, (c + 1) * SWA_V_ROWS)
        if n == 0:
            v_t = jnp.concatenate([vp_ref[0, 0, v_rows, :], vc_ref[0, 0, v_rows, 0:blk]], axis=1)
        else:
            v_t = vc_ref[0, 0, v_rows, (n - 1) * blk:(n + 1) * blk]
        o_t = _dot(v_t, p)
        denom = o_t[SWA_HEAD_DIM:SWA_HEAD_DIM + 1] + jnp.exp2(sinks[c] - m)
        o = o_t[:SWA_HEAD_DIM] / denom
        for j in range(SWA_GROUP // 2):
            pair_t = jnp.concatenate([o[:, (2 * j) * blk:(2 * j + 1) * blk],
                                      o[:, (2 * j + 1) * blk:(2 * j + 2) * blk]], axis=0)
            col = (c * SWA_GROUP // 2 + j) * LANES
            o_ref[0, n * blk:(n + 1) * blk, col:col + LANES] = pair_t.T

    bands = [(c, n) for c in range(SWA_KV_HEADS) for n in range(n_blk)]
    score_lead, softmax_lead = 3, 1
    s_vals, p_vals = {}, {}
    for step in range(len(bands) + score_lead):
        done = step - score_lead
        if done >= 0:
            band_output(*bands[done], *p_vals.pop(done))
        if step < len(bands):
            s_vals[step] = band_scores(*bands[step])
        ready = step - (score_lead - softmax_lead)
        if 0 <= ready < len(bands):
            p_vals[ready] = band_softmax(*bands[ready], s_vals.pop(ready))


def _swa_call(qbt, kb, vbt, sinks):
    b, n_tiles, qh, t = qbt.shape
    s = kb.shape[1]
    ratio = t // SWA_BLOCK
    kw = SWA_KV_HEADS * LANES
    vh = SWA_KV_HEADS * SWA_V_ROWS
    prev_tile = lambda i: jnp.maximum(i - 1, 0)
    return pl.pallas_call(
        _swa_kernel,
        grid=(b, n_tiles),
        in_specs=[pl.BlockSpec(memory_space=pltpu.SMEM),
                  pl.BlockSpec((1, 1, qh, t), lambda bi, i: (bi, i, 0, 0)),
                  pl.BlockSpec((1, t, kw), lambda bi, i: (bi, i, 0)),
                  pl.BlockSpec((1, SWA_BLOCK, kw), lambda bi, i: (bi, jnp.maximum(i * ratio - 1, 0), 0)),
                  pl.BlockSpec((1, 1, vh, t), lambda bi, i: (bi, i, 0, 0)),
                  pl.BlockSpec((1, 1, vh, SWA_BLOCK), lambda bi, i: (bi, prev_tile(i), 0, ratio - 1))],
        out_specs=pl.BlockSpec((1, t, SWA_WIDTH), lambda bi, i: (bi, i, 0)),
        out_shape=jax.ShapeDtypeStruct((b, s, SWA_WIDTH), F32),
        scratch_shapes=[pltpu.VMEM((t + SWA_BLOCK, kw), BF16)],
        compiler_params=pltpu.CompilerParams(
            dimension_semantics=("arbitrary", "arbitrary"), vmem_limit_bytes=VMEM_LIMIT),
        name="swa_attn",
    )(sinks, qbt, kb, kb, vbt, vbt)


def _out_ffn_kernel(x_ref, oa_ref, ob_ref, ga_ref, gb_ref, wo_ref, g_ref, wg_ref, wu_ref, wd_ref, o_ref):
    na = _rms(oa_ref[...], ga_ref[0], MLA_WIDTH).astype(BF16)
    nb = _rms(ob_ref[...], gb_ref[0], SWA_WIDTH).astype(BF16)
    y = _dot(na, wo_ref[0, 0:MLA_WIDTH, :]) + _dot(nb, wo_ref[0, MLA_WIDTH:, :])
    x_new = []
    _interleave(_ffn_stages(x_ref[...] + y, g_ref[0], wg_ref, wu_ref, wd_ref, x_new))
    o_ref[...] = x_new[0]


def _out_ffn_call(x, oa, ob, layer, ga, gb, wo, ffn_consts):
    n = x.shape[0]
    tm = min(TM_ROWS, n)
    return pl.pallas_call(
        _out_ffn_kernel,
        grid=(n // tm,),
        in_specs=[_row_spec(tm, D_MODEL), _row_spec(tm, MLA_WIDTH), _row_spec(tm, SWA_WIDTH),
                  _layer_spec(ga.shape, layer), _layer_spec(gb.shape, layer), _layer_spec(wo.shape, layer)]
                 + [_layer_spec(c.shape, layer) for c in ffn_consts],
        out_specs=_row_spec(tm, D_MODEL),
        out_shape=jax.ShapeDtypeStruct(x.shape, F32),
        compiler_params=pltpu.CompilerParams(
            dimension_semantics=("arbitrary",), vmem_limit_bytes=VMEM_LIMIT),
        name="out_ffn",
    )(x, oa, ob, ga, gb, wo, *ffn_consts)


def _pair_cols(n_heads, head_dim, base=0):
    half = head_dim // 2
    cols = []
    for pair in range(n_heads // 2):
        for part in range(2):
            for sub in range(2):
                start = base + (2 * pair + sub) * head_dim + part * half
                cols.extend(range(start, start + half))
    return np.asarray(cols, np.int32)


def _dup_cols(n_heads, head_dim, base=0):
    half = head_dim // 2
    cols = []
    for hd in range(n_heads):
        for part in range(2):
            start = base + hd * head_dim + part * half
            cols.extend(list(range(start, start + half)) * 2)
    return np.asarray(cols, np.int32)


def _proj_consts(mix_norm, w_in, mla_q_a_norm, mla_w_q_b, mla_kv_a_norm, mla_w_kv_b,
                 mla_q_norm, mla_k_norm, swa_q_norm, swa_k_norm):
    o_kpe = MLA_Q_RANK + MLA_KV_RANK
    o_qs = o_kpe + MLA_ROPE
    o_ks = o_qs + SWA_WIDTH
    o_vs = o_ks + SWA_KV_HEADS * SWA_HEAD_DIM
    in_cols = np.concatenate([
        np.arange(0, o_kpe, dtype=np.int32),
        _dup_cols(1, MLA_ROPE, o_kpe),
        _dup_cols(SWA_KV_HEADS, SWA_HEAD_DIM, o_ks)])
    ft_cols = np.concatenate([
        np.arange(o_qs, o_ks, dtype=np.int32),
        np.arange(o_vs, o_vs + SWA_KV_HEADS * SWA_HEAD_DIM, dtype=np.int32)])
    k_cols = np.concatenate(
        [np.arange(h * (MLA_NOPE + MLA_V), h * (MLA_NOPE + MLA_V) + MLA_NOPE) for h in range(MLA_HEADS)])
    v_cols = np.concatenate(
        [np.arange(h * (MLA_NOPE + MLA_V) + MLA_NOPE, (h + 1) * (MLA_NOPE + MLA_V)) for h in range(MLA_HEADS)])
    rope_gain = _dup_cols(1, MLA_ROPE, MLA_NOPE)
    swa_gain = _dup_cols(1, SWA_HEAD_DIM)
    t = lambda w: jnp.swapaxes(w, 1, 2)
    lane_rep = lambda v: jnp.broadcast_to(v.astype(F32)[:, :, None], v.shape + (LANES,))
    return (
        _rows(mix_norm),
        w_in[:, :, in_cols].astype(BF16),
        t(w_in[:, :, ft_cols]).astype(BF16),
        _rows(mla_q_a_norm),
        t(mla_w_q_b).astype(BF16),
        _rows(mla_kv_a_norm),
        mla_w_kv_b[:, :, k_cols].astype(BF16),
        t(mla_w_kv_b[:, :, v_cols]).astype(BF16),
        lane_rep(mla_q_norm),
        _rows(mla_k_norm[:, :MLA_NOPE]),
        _rows(mla_k_norm[:, rope_gain]),
        lane_rep(swa_q_norm),
        _rows(swa_k_norm[:, swa_gain]),
    )


def _rows(v):
    return v[:, None, :].astype(F32)


def _rope_tables(seq):
    pos = jnp.arange(seq, dtype=F32)
    inv = 1.0 / (ROPE_THETA ** (jnp.arange(0, 2 * HALF, 2, dtype=F32) / (2 * HALF)))
    ang = pos[:, None] * inv[None, :]
    c, s = jnp.cos(ang), jnp.sin(ang)
    return (jnp.concatenate([c, c, c, c], axis=1), jnp.concatenate([-s, -s, s, s], axis=1), c.T, s.T)


def kernel(x, ffn1_norm, ffn1_w_gate, ffn1_w_up, ffn1_w_down, mix_norm, w_in, mla_q_a_norm, mla_w_q_b, mla_kv_a_norm, mla_w_kv_b, mla_q_norm, mla_k_norm, swa_q_norm, swa_k_norm, swa_sinks, mla_out_norm, swa_out_norm, w_o, ffn2_norm, ffn2_w_gate, ffn2_w_up, ffn2_w_down):
    b, s, d = x.shape
    depth = w_in.shape[0]
    tables = _rope_tables(s)
    ffn1 = (_rows(ffn1_norm), ffn1_w_gate.astype(BF16), ffn1_w_up.astype(BF16), ffn1_w_down.astype(BF16))
    ffn2 = (_rows(ffn2_norm), ffn2_w_gate.astype(BF16), ffn2_w_up.astype(BF16), ffn2_w_down.astype(BF16))
    consts = _proj_consts(mix_norm, w_in, mla_q_a_norm, mla_w_q_b, mla_kv_a_norm, mla_w_kv_b,
                          mla_q_norm, mla_k_norm, swa_q_norm, swa_k_norm)
    out_consts = (_rows(mla_out_norm), _rows(swa_out_norm), w_o.astype(BF16))
    xf = x.reshape(b * s, d)
    for l in range(depth):
        xf, qt, ka, vt, qbt, kb, vbt = _ffn_proj_call(xf, s, l, ffn1, consts, tables)
        shp = lambda t: t.reshape(b, s, t.shape[-1])
        tiles = lambda t: t.reshape(b, -1, t.shape[-2], t.shape[-1])
        oa = _mla_call(tiles(qt), shp(ka), tiles(vt))
        ob = _swa_call(tiles(qbt), shp(kb), tiles(vbt), swa_sinks[l].astype(F32))
        xf = _out_ffn_call(xf, oa.reshape(b * s, MLA_WIDTH), ob.reshape(b * s, SWA_WIDTH), l,
                           *out_consts, ffn2)
    return xf.reshape(b, s, d)
```

```python
import functools

import numpy as np
import jax
import jax.numpy as jnp
from jax import lax
from jax.experimental import pallas as pl
from jax.experimental.pallas import tpu as pltpu

D_MODEL = 1024
EPS = 1e-6
ROPE_THETA = 10000.0
MLA_HEADS = 4
MLA_Q_RANK = 256
MLA_KV_RANK = 128
MLA_NOPE = 128
MLA_ROPE = 64
MLA_V = 128
MLA_QK = MLA_NOPE + MLA_ROPE
MLA_WIDTH = MLA_HEADS * MLA_V
SWA_HEADS = 8
SWA_KV_HEADS = 2
SWA_GROUP = SWA_HEADS // SWA_KV_HEADS
SWA_HEAD_DIM = 64
SWA_BLOCK = 128
SWA_WIDTH = SWA_HEADS * SWA_HEAD_DIM
D_FF = 2816

LANES = 128
HALF = SWA_HEAD_DIM // 2
MLA_QK_PAD = 2 * LANES
BF16_ROWS = 16
MLA_V_ROWS = MLA_V + BF16_ROWS
NEG_BIG = -1e30

VMEM_LIMIT = 56 * 1024 * 1024

LOG2E = 1.4426950408889634

TM_ROWS = 512
SWA_TQ = 512

BF16 = jnp.bfloat16
F32 = jnp.float32


def _rms(t, gain, width):
    ss = jnp.sum(t * t, axis=-1, keepdims=True)
    return t * lax.rsqrt(ss * (1.0 / width) + EPS) * gain


def _dot(a, b):
    return jnp.dot(a, b, preferred_element_type=F32)


def _dot_nt(a, b):
    return lax.dot_general(a, b, (((1,), (1,)), ((), ())), preferred_element_type=F32)


def _ffn_stages(x, g, wg_ref, wu_ref, wd_ref, out):
    h = _rms(x, g, D_MODEL).astype(BF16)
    gate = _dot(h, wg_ref[0])
    up = _dot(h, wu_ref[0])
    yield
    act = (gate * jax.nn.sigmoid(gate) * up).astype(BF16)
    out.append(x + 0.5 * _dot(act, wd_ref[0]))


def _interleave(*staged):
    active = list(staged)
    while active:
        for gen in list(active):
            try:
                next(gen)
            except StopIteration:
                active.remove(gen)


def _layer_spec(stacked_shape, layer):
    nd = len(stacked_shape)
    return pl.BlockSpec((1,) + tuple(stacked_shape[1:]), lambda *_: (layer,) + (0,) * (nd - 1),
                        pipeline_mode=pl.Buffered(1))


C_CQ = 0
C_CKV = C_CQ + MLA_Q_RANK
C_KPE = C_CKV + MLA_KV_RANK
C_KS = C_KPE + LANES
C_END = C_KS + SWA_KV_HEADS * LANES
R_VS = SWA_HEADS * SWA_HEAD_DIM
R_END = R_VS + SWA_KV_HEADS * SWA_HEAD_DIM
SWA_V_ROWS = SWA_HEAD_DIM + BF16_ROWS


def _rope(t, cos, sin_signed):
    return t * cos + pltpu.roll(t, 2 * HALF, 1) * sin_signed


def _proj_stages(x, g_ref, win_ref, wft_ref, gqa_ref, wqbt_ref, gkva_ref, wkb_ref, wvt_ref,
                 gq_ref, gkn_ref, gkr_ref, gsq_ref, gsk_ref, cos_ref, sin_ref, cost_ref, sint_ref,
                 qt_ref, ka_ref, vt_ref, qbt_ref, kb_ref, vbt_ref):
    tm = x.shape[0]
    q_scale = MLA_QK ** -0.5 * LOG2E
    swa_scale = SWA_HEAD_DIM ** -0.5 * LOG2E

    h = _rms(x, g_ref[0], D_MODEL).astype(BF16)
    proj = _dot(h, win_ref[0])
    proj_t = _dot_nt(wft_ref[0], h)
    yield

    cq_t = _rms(proj[:, C_CQ:C_CKV], gqa_ref[0], MLA_Q_RANK).T.astype(BF16)
    ckv = _rms(proj[:, C_CKV:C_KPE], gkva_ref[0], MLA_KV_RANK)
    q_up = _dot(wqbt_ref[0], cq_t)
    k_up = _dot(ckv.astype(BF16), wkb_ref[0])
    v_t = _dot(wvt_ref[0], ckv.T.astype(BF16)).astype(BF16)
    yield

    cos = cos_ref[...]
    sin = sin_ref[...]
    cos_t = cost_ref[...]
    sin_t = sint_ref[...]
    zeros = jnp.zeros((HALF, tm), BF16)
    ones_row = (lax.broadcasted_iota(jnp.int32, (BF16_ROWS, tm), 0) == 0).astype(BF16)

    gq = jnp.tile(gq_ref[0], (1, tm // LANES))
    for hd in range(MLA_HEADS):
        blk = q_up[hd * MLA_QK:(hd + 1) * MLA_QK]
        ss = jnp.sum(blk * blk, axis=0, keepdims=True)
        qn = blk * (lax.rsqrt(ss * (1.0 / MLA_QK) + EPS) * q_scale) * gq
        t1 = qn[MLA_NOPE:MLA_NOPE + HALF]
        t2 = qn[MLA_NOPE + HALF:]
        base = hd * MLA_QK_PAD
        qt_ref[0, base:base + MLA_NOPE, :] = qn[:MLA_NOPE].astype(BF16)
        qt_ref[0, base + MLA_NOPE:base + MLA_NOPE + HALF, :] = (t1 * cos_t - t2 * sin_t).astype(BF16)
        qt_ref[0, base + MLA_NOPE + HALF:base + MLA_NOPE + 2 * HALF, :] = zeros
        qt_ref[0, base + MLA_NOPE + 2 * HALF:base + MLA_NOPE + 3 * HALF, :] = (
            t2 * cos_t + t1 * sin_t).astype(BF16)
        qt_ref[0, base + MLA_NOPE + 3 * HALF:base + MLA_QK_PAD, :] = zeros

    for hd in range(MLA_HEADS):
        vt_ref[0, hd * MLA_V_ROWS:hd * MLA_V_ROWS + MLA_V, :] = v_t[hd * MLA_V:(hd + 1) * MLA_V]
        vt_ref[0, hd * MLA_V_ROWS + MLA_V:(hd + 1) * MLA_V_ROWS, :] = ones_row
    kpe = proj[:, C_KPE:C_KS]
    ss_pe = 0.5 * jnp.sum(kpe * kpe, axis=-1, keepdims=True)
    kpe_rot = _rope(kpe * gkr_ref[0], cos, sin)
    for hd in range(MLA_HEADS):
        nope = k_up[:, hd * MLA_NOPE:(hd + 1) * MLA_NOPE]
        ss = jnp.sum(nope * nope, axis=-1, keepdims=True) + ss_pe
        r = lax.rsqrt(ss * (1.0 / MLA_QK) + EPS)
        ka_ref[:, hd * MLA_QK_PAD: hd * MLA_QK_PAD + MLA_NOPE] = (
            nope * r * gkn_ref[0]).astype(BF16)
        ka_ref[:, hd * MLA_QK_PAD + MLA_NOPE:(hd + 1) * MLA_QK_PAD] = (kpe_rot * r).astype(BF16)

    gsq = jnp.tile(gsq_ref[0], (1, tm // LANES))
    for hd in range(SWA_HEADS):
        blk = proj_t[hd * SWA_HEAD_DIM:(hd + 1) * SWA_HEAD_DIM]
        ss = jnp.sum(blk * blk, axis=0, keepdims=True)
        qn = blk * (lax.rsqrt(ss * (1.0 / SWA_HEAD_DIM) + EPS) * swa_scale) * gsq
        t1 = qn[:HALF]
        t2 = qn[HALF:]
        base = hd * LANES
        qbt_ref[0, base:base + HALF, :] = (t1 * cos_t - t2 * sin_t).astype(BF16)
        qbt_ref[0, base + HALF:base + 2 * HALF, :] = zeros
        qbt_ref[0, base + 2 * HALF:base + 3 * HALF, :] = (t2 * cos_t + t1 * sin_t).astype(BF16)
        qbt_ref[0, base + 3 * HALF:base + LANES, :] = zeros
    for c in range(SWA_KV_HEADS):
        vbt_ref[0, c * SWA_V_ROWS:c * SWA_V_ROWS + SWA_HEAD_DIM, :] = (
            proj_t[R_VS + c * SWA_HEAD_DIM:R_VS + (c + 1) * SWA_HEAD_DIM].astype(BF16))
        vbt_ref[0, c * SWA_V_ROWS + SWA_HEAD_DIM:(c + 1) * SWA_V_ROWS, :] = ones_row
    for c in range(SWA_KV_HEADS):
        xk = proj[:, C_KS + c * LANES: C_KS + (c + 1) * LANES]
        ss = 0.5 * jnp.sum(xk * xk, axis=-1, keepdims=True)
        r = lax.rsqrt(ss * (1.0 / SWA_HEAD_DIM) + EPS)
        kb_ref[:, c * LANES:(c + 1) * LANES] = _rope(xk * r * gsk_ref[0], cos, sin).astype(BF16)


N_FFN_CONSTS = 4
N_PROJ_CONSTS = 13
N_PROJ_TABLES = 4


def _ffn_proj_kernel(*refs):
    x_ref = refs[0]
    ffn_refs = refs[1:1 + N_FFN_CONSTS]
    proj_refs = refs[1 + N_FFN_CONSTS:1 + N_FFN_CONSTS + N_PROJ_CONSTS + N_PROJ_TABLES]
    o_ref = refs[1 + N_FFN_CONSTS + N_PROJ_CONSTS + N_PROJ_TABLES]
    proj_out_refs = refs[-7:-1]
    x_prev = refs[-1]

    @pl.when(pl.program_id(0) == 0)
    def _():
        x_prev[...] = jnp.zeros_like(x_prev)

    g_ref, wg_ref, wu_ref, wd_ref = ffn_refs
    x_new = []
    _interleave(_proj_stages(x_prev[...], *proj_refs, *proj_out_refs),
                _ffn_stages(x_ref[...], g_ref[0], wg_ref, wu_ref, wd_ref, x_new))
    o_ref[...] = x_new[0]
    x_prev[...] = x_new[0]


def _ffn_proj_call(x, seq, layer, ffn_consts, consts, tables):
    n = x.shape[0]
    tm = min(TM_ROWS, seq)
    n_tiles = n // tm
    n_seq_tiles = seq // tm
    ffn_tile = lambda i: jnp.minimum(i, n_tiles - 1)
    proj_tile = lambda i: jnp.maximum(i - 1, 0)
    proj_row_spec = lambda width: pl.BlockSpec((tm, width), lambda i: (proj_tile(i), 0))
    ffn_row_spec = pl.BlockSpec((tm, D_MODEL), lambda i: (ffn_tile(i), 0))
    pos_spec = pl.BlockSpec((tm, LANES), lambda i: (proj_tile(i) % n_seq_tiles, 0))
    pos_t_spec = pl.BlockSpec((HALF, tm), lambda i: (0, proj_tile(i) % n_seq_tiles))
    row_widths = {1: MLA_HEADS * MLA_QK_PAD, 4: SWA_KV_HEADS * LANES}
    col_heights = {0: MLA_HEADS * MLA_QK_PAD, 2: MLA_HEADS * MLA_V_ROWS,
                   3: SWA_HEADS * LANES, 5: SWA_KV_HEADS * SWA_V_ROWS}
    out_specs, out_shape = [ffn_row_spec], [jax.ShapeDtypeStruct(x.shape, F32)]
    for idx in range(6):
        if idx in row_widths:
            out_specs.append(proj_row_spec(row_widths[idx]))
            out_shape.append(jax.ShapeDtypeStruct((n, row_widths[idx]), BF16))
        else:
            out_specs.append(pl.BlockSpec((1, col_heights[idx], tm), lambda i: (proj_tile(i), 0, 0)))
            out_shape.append(jax.ShapeDtypeStruct((n_tiles, col_heights[idx], tm), BF16))
    assert len(ffn_consts) == N_FFN_CONSTS and len(consts) == N_PROJ_CONSTS and len(tables) == N_PROJ_TABLES
    return pl.pallas_call(
        _ffn_proj_kernel,
        grid=(n_tiles + 1,),
        in_specs=([ffn_row_spec] + [_layer_spec(c.shape, layer) for c in ffn_consts]
                  + [_layer_spec(c.shape, layer) for c in consts]
                  + [pos_spec, pos_spec, pos_t_spec, pos_t_spec]),
        out_specs=out_specs,
        out_shape=out_shape,
        scratch_shapes=[pltpu.VMEM((tm, D_MODEL), F32)],
        compiler_params=pltpu.CompilerParams(
            dimension_semantics=("arbitrary",), vmem_limit_bytes=VMEM_LIMIT),
        name="ffn_proj",
    )(x, *ffn_consts, *consts, *tables)


MLA_STREAMS = 4


def _mla_kernel(q_ref, k_ref, v_ref, o_ref, s0_ref, s1_ref, p0_ref, p1_ref, acc_ref):
    t = q_ref.shape[3]
    tk = t // 2
    qi = pl.program_id(2)
    heads = range(MLA_STREAMS)
    s_refs = (s0_ref, s1_ref)
    p_refs = (p0_ref, p1_ref)

    def scores(tile, half, h):
        start = pl.multiple_of(tile * t, t) + half * tk
        k = k_ref[0, pl.ds(start, tk), h * MLA_QK_PAD:(h + 1) * MLA_QK_PAD]
        return _dot(k, q_ref[0, 0, h * MLA_QK_PAD:(h + 1) * MLA_QK_PAD, :])

    def weighted_values(tile, half, h, buf):
        v = v_ref[0, tile, h * MLA_V_ROWS:(h + 1) * MLA_V_ROWS, half * tk:(half + 1) * tk]
        return _dot(v, p_refs[buf][h, :, :t])

    def phase(cur, state, nxt_tile, prev_tile, mask_half=None):
        new_state = []
        for h in heads:
            m, alpha_prev = state[h]
            if prev_tile is not None:
                pv = weighted_values(prev_tile, 1 - cur, h, 1 - cur)
            if nxt_tile is not None:
                s_refs[1 - cur][h, :, :t] = scores(nxt_tile, 1 - cur, h)
            s = s_refs[cur][h, :, :t]
            if mask_half is not None:
                key = lax.broadcasted_iota(jnp.int32, (tk, t), 0) + mask_half * tk
                qry = lax.broadcasted_iota(jnp.int32, (tk, t), 1)
                s = jnp.where(key <= qry, s, NEG_BIG)
            m_new = jnp.maximum(m, jnp.max(s, axis=0, keepdims=True))
            p_refs[cur][h, :, :t] = jnp.exp2(s - m_new).astype(BF16)
            alpha = jnp.exp2(m - m_new)
            if prev_tile is not None:
                acc_ref[h, :, :t] = alpha_prev * acc_ref[h, :, :t] + pv
            new_state.append((m_new, alpha))
        return tuple(new_state)

    state = tuple((jnp.full((1, t), NEG_BIG, F32), jnp.zeros((1, t), F32)) for _ in heads)
    for h in heads:
        acc_ref[h, :, :t] = jnp.zeros((MLA_V_ROWS, t), F32)
        s_refs[0][h, :, :t] = scores(qi, 0, h)
    state = phase(0, state, qi, None, mask_half=0)
    state = phase(1, state, 0, qi, mask_half=1)

    def key_tile(jj, state):
        state = phase(0, state, jj, jnp.where(jj == 0, qi, jj - 1))
        return phase(1, state, jj + 1, jj)

    state = lax.fori_loop(0, qi // 2, lambda kk, st: key_tile(2 * kk + 1, key_tile(2 * kk, st)), state)
    state = lax.cond(qi % 2 == 1, lambda st: key_tile(qi - 1, st), lambda st: st, state)
    last_tile = jnp.maximum(qi - 1, 0)
    for h in heads:
        m, alpha_prev = state[h]
        acc = alpha_prev * acc_ref[h, :, :t] + weighted_values(last_tile, 1, h, 1)
        o_ref[0, :, h * MLA_V:(h + 1) * MLA_V] = (acc[:MLA_V] / acc[MLA_V:MLA_V + 1]).T


def _mla_call(qt, ka, vt):
    b, nq, _, t = qt.shape
    s = ka.shape[1]
    qw = MLA_STREAMS * MLA_QK_PAD
    vw = MLA_STREAMS * MLA_V_ROWS
    ow = MLA_STREAMS * MLA_V
    return pl.pallas_call(
        _mla_kernel,
        grid=(b, MLA_HEADS // MLA_STREAMS, nq),
        in_specs=[pl.BlockSpec((1, 1, qw, t), lambda bi, h, i: (bi, i, h, 0)),
                  pl.BlockSpec((1, s, qw), lambda bi, h, i: (bi, 0, h), pipeline_mode=pl.Buffered(1)),
                  pl.BlockSpec((1, nq, vw, t), lambda bi, h, i: (bi, 0, h, 0),
                               pipeline_mode=pl.Buffered(1))],
        out_specs=pl.BlockSpec((1, t, ow), lambda bi, h, i: (bi, i, h)),
        out_shape=jax.ShapeDtypeStruct((b, s, MLA_WIDTH), F32),
        scratch_shapes=[pltpu.VMEM((MLA_STREAMS, t // 2, t + LANES), F32),
                        pltpu.VMEM((MLA_STREAMS, t // 2, t + LANES), F32),
                        pltpu.VMEM((MLA_STREAMS, t // 2, t + LANES), BF16),
                        pltpu.VMEM((MLA_STREAMS, t // 2, t + LANES), BF16),
                        pltpu.VMEM((MLA_STREAMS, MLA_V_ROWS, t + LANES), F32)],
        compiler_params=pltpu.CompilerParams(
            dimension_semantics=("arbitrary", "arbitrary", "arbitrary"),
            vmem_limit_bytes=VMEM_LIMIT),
        name="mla_attn",
    )(qt, ka, vt)


def _swa_kernel(sink_ref, q_ref, kc_ref, kp_ref, vc_ref, vp_ref, o_ref, kf_ref):
    blk = SWA_BLOCK
    t = q_ref.shape[3]
    n_blk = t // blk
    tile = pl.program_id(1)
    kf_ref[0:blk, :] = kp_ref[0]
    kf_ref[blk:, :] = kc_ref[0]

    cols = SWA_GROUP * blk
    k_rel = lax.broadcasted_iota(jnp.int32, (2 * blk, cols), 0)
    q_rel = lax.broadcasted_iota(jnp.int32, (2 * blk, cols), 1) & (blk - 1)
    in_window = (k_rel > q_rel) & (k_rel <= q_rel + blk)
    head_of_col = lax.broadcasted_iota(jnp.int32, (1, cols), 1) // blk

    sinks = []
    for c in range(SWA_KV_HEADS):
        sink = jnp.zeros((1, cols), F32)
        for g in range(SWA_GROUP):
            sink = jnp.where(head_of_col == g, sink_ref[c * SWA_GROUP + g] * LOG2E, sink)
        sinks.append(sink)

    def band_scores(c, n):
        q_t = jnp.concatenate(
            [q_ref[0, 0, (c * SWA_GROUP + g) * LANES:(c * SWA_GROUP + g + 1) * LANES, n * blk:(n + 1) * blk]
             for g in range(SWA_GROUP)], axis=1)
        k = kf_ref[n * blk:(n + 2) * blk, c * LANES:(c + 1) * LANES]
        return _dot(k, q_t)

    def band_softmax(c, n, s):
        valid = in_window
        if n == 0:
            valid = valid & ((k_rel >= blk) | (tile > 0))
        s = jnp.where(valid, s, NEG_BIG)
        m = jnp.maximum(jnp.max(s, axis=0, keepdims=True), sinks[c])
        return jnp.exp2(s - m).astype(BF16), m

    def band_output(c, n, p, m):
        v_rows = slice(c * SWA_V_ROWS, (c + 1) * SWA_V_ROWS)
        if n == 0:
            v_t = jnp.concatenate([vp_ref[0, 0, v_rows, :], vc_ref[0, 0, v_rows, 0:blk]], axis=1)
        else:
            v_t = vc_ref[0, 0, v_rows, (n - 1) * blk:(n + 1) * blk]
        o_t = _dot(v_t, p)
        denom = o_t[SWA_HEAD_DIM:SWA_HEAD_DIM + 1] + jnp.exp2(sinks[c] - m)
        o = o_t[:SWA_HEAD_DIM] / denom
        for j in range(SWA_GROUP // 2):
            pair_t = jnp.concatenate([o[:, (2 * j) * blk:(2 * j + 1) * blk],
                                      o[:, (2 * j + 1) * blk:(2 * j + 2) * blk]], axis=0)
            col = (c * SWA_GROUP // 2 + j) * LANES
            o_ref[0, n * blk:(n + 1) * blk, col:col + LANES] = pair_t.T

    bands = [(c, n) for c in range(SWA_KV_HEADS) for n in range(n_blk)]
    score_lead, softmax_lead = 3, 1
    s_vals, p_vals = {}, {}
    for step in range(len(bands) + score_lead):
        done = step - score_lead
        if done >= 0:
            band_output(*bands[done], *p_vals.pop(done))
        if step < len(bands):
            s_vals[step] = band_scores(*bands[step])
        ready = step - (score_lead - softmax_lead)
        if 0 <= ready < len(bands):
            p_vals[ready] = band_softmax(*bands[ready], s_vals.pop(ready))


def _swa_call(qbt, kb, vbt, sinks):
    b, n_tiles, qh, t = qbt.shape
    s = kb.shape[1]
    ratio = t // SWA_BLOCK
    kw = SWA_KV_HEADS * LANES
    vh = SWA_KV_HEADS * SWA_V_ROWS
    prev_tile = lambda i: jnp.maximum(i - 1, 0)
    return pl.pallas_call(
        _swa_kernel,
        grid=(b, n_tiles),
        in_specs=[pl.BlockSpec(memory_space=pltpu.SMEM),
                  pl.BlockSpec((1, 1, qh, t), lambda bi, i: (bi, i, 0, 0)),
                  pl.BlockSpec((1, t, kw), lambda bi, i: (bi, i, 0)),
                  pl.BlockSpec((1, SWA_BLOCK, kw), lambda bi, i: (bi, jnp.maximum(i * ratio - 1, 0), 0)),
                  pl.BlockSpec((1, 1, vh, t), lambda bi, i: (bi, i, 0, 0)),
                  pl.BlockSpec((1, 1, vh, SWA_BLOCK), lambda bi, i: (bi, prev_tile(i), 0, ratio - 1))],
        out_specs=pl.BlockSpec((1, t, SWA_WIDTH), lambda bi, i: (bi, i, 0)),
        out_shape=jax.ShapeDtypeStruct((b, s, SWA_WIDTH), F32),
        scratch_shapes=[pltpu.VMEM((t + SWA_BLOCK, kw), BF16)],
        compiler_params=pltpu.CompilerParams(
            dimension_semantics=("arbitrary", "arbitrary"), vmem_limit_bytes=VMEM_LIMIT),
        name="swa_attn",
    )(sinks, qbt, kb, kb, vbt, vbt)


def _out_ffn_kernel(x_ref, oa_ref, ob_ref, ga_ref, gb_ref, wo_ref, g_ref, wg_ref, wu_ref, wd_ref, o_ref):
    na = _rms(oa_ref[...], ga_ref[0], MLA_WIDTH).astype(BF16)
    nb = _rms(ob_ref[...], gb_ref[0], SWA_WIDTH).astype(BF16)
    y = _dot(na, wo_ref[0, 0:MLA_WIDTH, :]) + _dot(nb, wo_ref[0, MLA_WIDTH:, :])
    x_new = []
    _interleave(_ffn_stages(x_ref[...] + y, g_ref[0], wg_ref, wu_ref, wd_ref, x_new))
    o_ref[...] = x_new[0]


def _out_ffn_call(x, oa, ob, layer, ga, gb, wo, ffn_consts):
    n = x.shape[0]
    tm = min(TM_ROWS, n)
    row_spec = lambda width: pl.BlockSpec((tm, width), lambda i: (i, 0))
    return pl.pallas_call(
        _out_ffn_kernel,
        grid=(n // tm,),
        in_specs=[row_spec(D_MODEL), row_spec(MLA_WIDTH), row_spec(SWA_WIDTH),
                  _layer_spec(ga.shape, layer), _layer_spec(gb.shape, layer), _layer_spec(wo.shape, layer)]
                 + [_layer_spec(c.shape, layer) for c in ffn_consts],
        out_specs=row_spec(D_MODEL),
        out_shape=jax.ShapeDtypeStruct(x.shape, F32),
        compiler_params=pltpu.CompilerParams(
            dimension_semantics=("arbitrary",), vmem_limit_bytes=VMEM_LIMIT),
        name="out_ffn",
    )(x, oa, ob, ga, gb, wo, *ffn_consts)


def _take_cols(w, cols):
    cols = np.asarray(cols)
    cuts = [0] + [i for i in range(1, len(cols)) if cols[i] != cols[i - 1] + 1] + [len(cols)]
    runs = [w[..., int(cols[a]):int(cols[b - 1]) + 1] for a, b in zip(cuts[:-1], cuts[1:])]
    return runs[0] if len(runs) == 1 else jnp.concatenate(runs, axis=-1)


def _dup_cols(n_heads, head_dim, base=0):
    half = head_dim // 2
    cols = []
    for hd in range(n_heads):
        for part in range(2):
            start = base + hd * head_dim + part * half
            cols.extend(list(range(start, start + half)) * 2)
    return np.asarray(cols, np.int32)


def _proj_consts(mix_norm, w_in, mla_q_a_norm, mla_w_q_b, mla_kv_a_norm, mla_w_kv_b,
                 mla_q_norm, mla_k_norm, swa_q_norm, swa_k_norm):
    o_kpe = MLA_Q_RANK + MLA_KV_RANK
    o_qs = o_kpe + MLA_ROPE
    o_ks = o_qs + SWA_WIDTH
    o_vs = o_ks + SWA_KV_HEADS * SWA_HEAD_DIM
    in_cols = np.concatenate([
        np.arange(0, o_kpe, dtype=np.int32),
        _dup_cols(1, MLA_ROPE, o_kpe),
        _dup_cols(SWA_KV_HEADS, SWA_HEAD_DIM, o_ks)])
    ft_cols = np.concatenate([
        np.arange(o_qs, o_ks, dtype=np.int32),
        np.arange(o_vs, o_vs + SWA_KV_HEADS * SWA_HEAD_DIM, dtype=np.int32)])
    k_cols = np.concatenate(
        [np.arange(h * (MLA_NOPE + MLA_V), h * (MLA_NOPE + MLA_V) + MLA_NOPE) for h in range(MLA_HEADS)])
    v_cols = np.concatenate(
        [np.arange(h * (MLA_NOPE + MLA_V) + MLA_NOPE, (h + 1) * (MLA_NOPE + MLA_V)) for h in range(MLA_HEADS)])
    rope_gain = _dup_cols(1, MLA_ROPE, MLA_NOPE)
    swa_gain = _dup_cols(1, SWA_HEAD_DIM)
    t = lambda w: jnp.swapaxes(w, 1, 2)
    lane_rep = lambda v: jnp.broadcast_to(v.astype(F32)[:, :, None], v.shape + (LANES,))
    return (
        _rows(mix_norm),
        _take_cols(w_in, in_cols).astype(BF16),
        t(_take_cols(w_in, ft_cols)).astype(BF16),
        _rows(mla_q_a_norm),
        t(mla_w_q_b).astype(BF16),
        _rows(mla_kv_a_norm),
        _take_cols(mla_w_kv_b, k_cols).astype(BF16),
        t(_take_cols(mla_w_kv_b, v_cols)).astype(BF16),
        lane_rep(mla_q_norm),
        _rows(mla_k_norm[:, :MLA_NOPE]),
        _rows(_take_cols(mla_k_norm, rope_gain)),
        lane_rep(swa_q_norm),
        _rows(_take_cols(swa_k_norm, swa_gain)),
    )


def _rows(v):
    return v[:, None, :].astype(F32)


def _rope_tables(seq):
    pos = jnp.arange(seq, dtype=F32)
    inv = 1.0 / (ROPE_THETA ** (jnp.arange(0, 2 * HALF, 2, dtype=F32) / (2 * HALF)))
    ang = pos[:, None] * inv[None, :]
    c, s = jnp.cos(ang), jnp.sin(ang)
    return (jnp.concatenate([c, c, c, c], axis=1), jnp.concatenate([-s, -s, s, s], axis=1), c.T, s.T)


def kernel(x, ffn1_norm, ffn1_w_gate, ffn1_w_up, ffn1_w_down, mix_norm, w_in, mla_q_a_norm, mla_w_q_b, mla_kv_a_norm, mla_w_kv_b, mla_q_norm, mla_k_norm, swa_q_norm, swa_k_norm, swa_sinks, mla_out_norm, swa_out_norm, w_o, ffn2_norm, ffn2_w_gate, ffn2_w_up, ffn2_w_down):
    b, s, d = x.shape
    depth = w_in.shape[0]
    tables = _rope_tables(s)
    ffn1 = (_rows(ffn1_norm), ffn1_w_gate.astype(BF16), ffn1_w_up.astype(BF16), ffn1_w_down.astype(BF16))
    ffn2 = (_rows(ffn2_norm), ffn2_w_gate.astype(BF16), ffn2_w_up.astype(BF16), ffn2_w_down.astype(BF16))
    consts = _proj_consts(mix_norm, w_in, mla_q_a_norm, mla_w_q_b, mla_kv_a_norm, mla_w_kv_b,
                          mla_q_norm, mla_k_norm, swa_q_norm, swa_k_norm)
    out_consts = (_rows(mla_out_norm), _rows(swa_out_norm), w_o.astype(BF16))
    xf = x.reshape(b * s, d)
    for l in range(depth):
        xf, qt, ka, vt, qbt, kb, vbt = _ffn_proj_call(xf, s, l, ffn1, consts, tables)
        shp = lambda t: t.reshape(b, s, t.shape[-1])
        tiles = lambda t: t.reshape(b, -1, t.shape[-2], t.shape[-1])
        oa = _mla_call(tiles(qt), shp(ka), tiles(vt))
        ob = _swa_call(tiles(qbt), shp(kb), tiles(vbt), swa_sinks[l].astype(F32))
        xf = _out_ffn_call(xf, oa.reshape(b * s, MLA_WIDTH), ob.reshape(b * s, SWA_WIDTH), l,
                           *out_consts, ffn2)
    return xf.reshape(b, s, d)
```

```python
import functools

import numpy as np
import jax
import jax.numpy as jnp
from jax import lax
from jax.experimental import pallas as pl
from jax.experimental.pallas import tpu as pltpu

D_MODEL = 1024
EPS = 1e-6
ROPE_THETA = 10000.0
MLA_HEADS = 4
MLA_Q_RANK = 256
MLA_KV_RANK = 128
MLA_NOPE = 128
MLA_ROPE = 64
MLA_V = 128
MLA_QK = MLA_NOPE + MLA_ROPE
MLA_WIDTH = MLA_HEADS * MLA_V
SWA_HEADS = 8
SWA_KV_HEADS = 2
SWA_GROUP = SWA_HEADS // SWA_KV_HEADS
SWA_HEAD_DIM = 64
SWA_BLOCK = 128
SWA_WIDTH = SWA_HEADS * SWA_HEAD_DIM
D_FF = 2816

LANES = 128
HALF = SWA_HEAD_DIM // 2
MLA_QK_PAD = 2 * LANES
BF16_ROWS = 16
MLA_V_ROWS = MLA_V + BF16_ROWS
NEG_BIG = -1e30

VMEM_LIMIT = 56 * 1024 * 1024

LOG2E = 1.4426950408889634

TM_ROWS = 512
SWA_TQ = 512

BF16 = jnp.bfloat16
F32 = jnp.float32


def _rms(t, gain, width):
    ss = jnp.sum(t * t, axis=-1, keepdims=True)
    return t * lax.rsqrt(ss * (1.0 / width) + EPS) * gain


def _dot(a, b):
    return jnp.dot(a, b, preferred_element_type=F32)


def _dot_nt(a, b):
    return lax.dot_general(a, b, (((1,), (1,)), ((), ())), preferred_element_type=F32)


def _ffn_stages(x, g, wg_ref, wu_ref, wd_ref, out):
    h = _rms(x, g, D_MODEL).astype(BF16)
    gate = _dot(h, wg_ref[0])
    up = _dot(h, wu_ref[0])
    yield
    act = (gate * jax.nn.sigmoid(gate) * up).astype(BF16)
    out.append(x + 0.5 * _dot(act, wd_ref[0]))


def _interleave(*staged):
    active = list(staged)
    while active:
        for gen in list(active):
            try:
                next(gen)
            except StopIteration:
                active.remove(gen)


def _layer_spec(stacked_shape, layer):
    nd = len(stacked_shape)
    return pl.BlockSpec((1,) + tuple(stacked_shape[1:]), lambda *_: (layer,) + (0,) * (nd - 1),
                        pipeline_mode=pl.Buffered(1))


C_CQ = 0
C_CKV = C_CQ + MLA_Q_RANK
C_KPE = C_CKV + MLA_KV_RANK
C_KS = C_KPE + LANES
C_END = C_KS + SWA_KV_HEADS * LANES
R_VS = SWA_HEADS * SWA_HEAD_DIM
R_END = R_VS + SWA_KV_HEADS * SWA_HEAD_DIM
SWA_V_ROWS = SWA_HEAD_DIM + BF16_ROWS


def _rope(t, cos, sin_signed):
    return t * cos + pltpu.roll(t, 2 * HALF, 1) * sin_signed


def _proj_stages(x, g_ref, win_ref, wft_ref, gqa_ref, wqbt_ref, gkva_ref, wkb_ref, wvt_ref,
                 gq_ref, gkn_ref, gkr_ref, gsq_ref, gsk_ref, cos_ref, sin_ref, cost_ref, sint_ref,
                 qt_ref, ka_ref, vt_ref, qbt_ref, kb_ref, vbt_ref):
    tm = x.shape[0]
    q_scale = MLA_QK ** -0.5 * LOG2E
    swa_scale = SWA_HEAD_DIM ** -0.5 * LOG2E

    h = _rms(x, g_ref[0], D_MODEL).astype(BF16)
    proj = _dot(h, win_ref[0])
    proj_t = _dot_nt(wft_ref[0], h)
    yield

    cq_t = _rms(proj[:, C_CQ:C_CKV], gqa_ref[0], MLA_Q_RANK).T.astype(BF16)
    ckv = _rms(proj[:, C_CKV:C_KPE], gkva_ref[0], MLA_KV_RANK)
    q_up = _dot(wqbt_ref[0], cq_t)
    k_up = _dot(ckv.astype(BF16), wkb_ref[0])
    v_t = _dot(wvt_ref[0], ckv.T.astype(BF16)).astype(BF16)
    yield

    cos = cos_ref[...]
    sin = sin_ref[...]
    cos_t = cost_ref[...]
    sin_t = sint_ref[...]
    zeros = jnp.zeros((HALF, tm), BF16)
    ones_row = (lax.broadcasted_iota(jnp.int32, (BF16_ROWS, tm), 0) == 0).astype(BF16)

    gq = jnp.tile(gq_ref[0], (1, tm // LANES))
    for hd in range(MLA_HEADS):
        blk = q_up[hd * MLA_QK:(hd + 1) * MLA_QK]
        ss = jnp.sum(blk * blk, axis=0, keepdims=True)
        qn = blk * (lax.rsqrt(ss * (1.0 / MLA_QK) + EPS) * q_scale) * gq
        t1 = qn[MLA_NOPE:MLA_NOPE + HALF]
        t2 = qn[MLA_NOPE + HALF:]
        base = hd * MLA_QK_PAD
        qt_ref[0, base:base + MLA_NOPE, :] = qn[:MLA_NOPE].astype(BF16)
        qt_ref[0, base + MLA_NOPE:base + MLA_NOPE + HALF, :] = (t1 * cos_t - t2 * sin_t).astype(BF16)
        qt_ref[0, base + MLA_NOPE + HALF:base + MLA_NOPE + 2 * HALF, :] = zeros
        qt_ref[0, base + MLA_NOPE + 2 * HALF:base + MLA_NOPE + 3 * HALF, :] = (
            t2 * cos_t + t1 * sin_t).astype(BF16)
        qt_ref[0, base + MLA_NOPE + 3 * HALF:base + MLA_QK_PAD, :] = zeros

    for hd in range(MLA_HEADS):
        vt_ref[0, hd * MLA_V_ROWS:hd * MLA_V_ROWS + MLA_V, :] = v_t[hd * MLA_V:(hd + 1) * MLA_V]
        vt_ref[0, hd * MLA_V_ROWS + MLA_V:(hd + 1) * MLA_V_ROWS, :] = ones_row
    kpe = proj[:, C_KPE:C_KS]
    ss_pe = 0.5 * jnp.sum(kpe * kpe, axis=-1, keepdims=True)
    kpe_rot = _rope(kpe * gkr_ref[0], cos, sin)
    for hd in range(MLA_HEADS):
        nope = k_up[:, hd * MLA_NOPE:(hd + 1) * MLA_NOPE]
        ss = jnp.sum(nope * nope, axis=-1, keepdims=True) + ss_pe
        r = lax.rsqrt(ss * (1.0 / MLA_QK) + EPS)
        ka_ref[:, hd * MLA_QK_PAD: hd * MLA_QK_PAD + MLA_NOPE] = (
            nope * r * gkn_ref[0]).astype(BF16)
        ka_ref[:, hd * MLA_QK_PAD + MLA_NOPE:(hd + 1) * MLA_QK_PAD] = (kpe_rot * r).astype(BF16)

    gsq = jnp.tile(gsq_ref[0], (1, tm // LANES))
    for hd in range(SWA_HEADS):
        blk = proj_t[hd * SWA_HEAD_DIM:(hd + 1) * SWA_HEAD_DIM]
        ss = jnp.sum(blk * blk, axis=0, keepdims=True)
        qn = blk * (lax.rsqrt(ss * (1.0 / SWA_HEAD_DIM) + EPS) * swa_scale) * gsq
        t1 = qn[:HALF]
        t2 = qn[HALF:]
        base = hd * LANES
        qbt_ref[0, base:base + HALF, :] = (t1 * cos_t - t2 * sin_t).astype(BF16)
        qbt_ref[0, base + HALF:base + 2 * HALF, :] = zeros
        qbt_ref[0, base + 2 * HALF:base + 3 * HALF, :] = (t2 * cos_t + t1 * sin_t).astype(BF16)
        qbt_ref[0, base + 3 * HALF:base + LANES, :] = zeros
    for c in range(SWA_KV_HEADS):
        vbt_ref[0, c * SWA_V_ROWS:c * SWA_V_ROWS + SWA_HEAD_DIM, :] = (
            proj_t[R_VS + c * SWA_HEAD_DIM:R_VS + (c + 1) * SWA_HEAD_DIM].astype(BF16))
        vbt_ref[0, c * SWA_V_ROWS + SWA_HEAD_DIM:(c + 1) * SWA_V_ROWS, :] = ones_row
    for c in range(SWA_KV_HEADS):
        xk = proj[:, C_KS + c * LANES: C_KS + (c + 1) * LANES]
        ss = 0.5 * jnp.sum(xk * xk, axis=-1, keepdims=True)
        r = lax.rsqrt(ss * (1.0 / SWA_HEAD_DIM) + EPS)
        kb_ref[:, c * LANES:(c + 1) * LANES] = _rope(xk * r * gsk_ref[0], cos, sin).astype(BF16)


N_FFN_CONSTS = 4
N_PROJ_CONSTS = 13
N_PROJ_TABLES = 4


def _ffn_proj_kernel(*refs):
    x_ref = refs[0]
    ffn_refs = refs[1:1 + N_FFN_CONSTS]
    proj_refs = refs[1 + N_FFN_CONSTS:1 + N_FFN_CONSTS + N_PROJ_CONSTS + N_PROJ_TABLES]
    o_ref = refs[1 + N_FFN_CONSTS + N_PROJ_CONSTS + N_PROJ_TABLES]
    proj_out_refs = refs[-7:-1]
    x_prev = refs[-1]

    @pl.when(pl.program_id(0) == 0)
    def _():
        x_prev[...] = jnp.zeros_like(x_prev)

    g_ref, wg_ref, wu_ref, wd_ref = ffn_refs
    x_new = []
    _interleave(_proj_stages(x_prev[...], *proj_refs, *proj_out_refs),
                _ffn_stages(x_ref[...], g_ref[0], wg_ref, wu_ref, wd_ref, x_new))
    o_ref[...] = x_new[0]
    x_prev[...] = x_new[0]


def _ffn_proj_call(x, seq, layer, ffn_consts, consts, tables):
    n = x.shape[0]
    tm = min(TM_ROWS, seq)
    n_tiles = n // tm
    n_seq_tiles = seq // tm
    ffn_tile = lambda i: jnp.minimum(i, n_tiles - 1)
    proj_tile = lambda i: jnp.maximum(i - 1, 0)
    proj_row_spec = lambda width: pl.BlockSpec((tm, width), lambda i: (proj_tile(i), 0))
    ffn_row_spec = pl.BlockSpec((tm, D_MODEL), lambda i: (ffn_tile(i), 0))
    pos_spec = pl.BlockSpec((tm, LANES), lambda i: (proj_tile(i) % n_seq_tiles, 0))
    pos_t_spec = pl.BlockSpec((HALF, tm), lambda i: (0, proj_tile(i) % n_seq_tiles))
    row_widths = {1: MLA_HEADS * MLA_QK_PAD, 4: SWA_KV_HEADS * LANES}
    col_heights = {0: MLA_HEADS * MLA_QK_PAD, 2: MLA_HEADS * MLA_V_ROWS,
                   3: SWA_HEADS * LANES, 5: SWA_KV_HEADS * SWA_V_ROWS}
    out_specs, out_shape = [ffn_row_spec], [jax.ShapeDtypeStruct(x.shape, F32)]
    for idx in range(6):
        if idx in row_widths:
            out_specs.append(proj_row_spec(row_widths[idx]))
            out_shape.append(jax.ShapeDtypeStruct((n, row_widths[idx]), BF16))
        else:
            out_specs.append(pl.BlockSpec((1, col_heights[idx], tm), lambda i: (proj_tile(i), 0, 0)))
            out_shape.append(jax.ShapeDtypeStruct((n_tiles, col_heights[idx], tm), BF16))
    assert len(ffn_consts) == N_FFN_CONSTS and len(consts) == N_PROJ_CONSTS and len(tables) == N_PROJ_TABLES
    return pl.pallas_call(
        _ffn_proj_kernel,
        grid=(n_tiles + 1,),
        in_specs=([ffn_row_spec] + [_layer_spec(c.shape, layer) for c in ffn_consts]
                  + [_layer_spec(c.shape, layer) for c in consts]
                  + [pos_spec, pos_spec, pos_t_spec, pos_t_spec]),
        out_specs=out_specs,
        out_shape=out_shape,
        scratch_shapes=[pltpu.VMEM((tm, D_MODEL), F32)],
        compiler_params=pltpu.CompilerParams(
            dimension_semantics=("arbitrary",), vmem_limit_bytes=VMEM_LIMIT),
        name="ffn_proj",
    )(x, *ffn_consts, *consts, *tables)


MLA_STREAMS = 4


def _mla_kernel(q_ref, k_tile_ref, v_tile_ref, o_ref, k_ref, v_ref, s0_ref, s1_ref, p0_ref, p1_ref, acc_ref):
    t = q_ref.shape[3]
    tk = t // 2
    qi = pl.program_id(2)
    heads = range(MLA_STREAMS)
    s_refs = (s0_ref, s1_ref)
    p_refs = (p0_ref, p1_ref)

    k_ref[pl.ds(pl.multiple_of(qi * t, t), t), :] = k_tile_ref[0]
    v_ref[qi] = v_tile_ref[0, 0]

    def scores(tile, half, h):
        start = pl.multiple_of(tile * t, t) + half * tk
        k = k_ref[pl.ds(start, tk), h * MLA_QK_PAD:(h + 1) * MLA_QK_PAD]
        return _dot(k, q_ref[0, 0, h * MLA_QK_PAD:(h + 1) * MLA_QK_PAD, :])

    def weighted_values(tile, half, h, buf):
        v = v_ref[tile, h * MLA_V_ROWS:(h + 1) * MLA_V_ROWS, half * tk:(half + 1) * tk]
        return _dot(v, p_refs[buf][h, :, :t])

    def phase(cur, state, nxt_tile, prev_tile, mask_half=None):
        new_state = []
        for h in heads:
            m, alpha_prev, block_max = state[h]
            if prev_tile is not None:
                pv = weighted_values(prev_tile, 1 - cur, h, 1 - cur)
            next_max = block_max
            if nxt_tile is not None:
                nxt = scores(nxt_tile, 1 - cur, h)
                s_refs[1 - cur][h, :, :t] = nxt
                next_max = jnp.max(nxt, axis=0, keepdims=True)
            s = s_refs[cur][h, :, :t]
            if mask_half is not None:
                key = lax.broadcasted_iota(jnp.int32, (tk, t), 0) + mask_half * tk
                qry = lax.broadcasted_iota(jnp.int32, (tk, t), 1)
                s = jnp.where(key <= qry, s, NEG_BIG)
                block_max = jnp.max(s, axis=0, keepdims=True)
            m_new = jnp.maximum(m, block_max)
            p_refs[cur][h, :, :t] = jnp.exp2(s - m_new).astype(BF16)
            alpha = jnp.exp2(m - m_new)
            if prev_tile is not None:
                acc_ref[h, :, :t] = alpha_prev * acc_ref[h, :, :t] + pv
            new_state.append((m_new, alpha, next_max))
        return tuple(new_state)

    state = tuple((jnp.full((1, t), NEG_BIG, F32), jnp.zeros((1, t), F32), jnp.zeros((1, t), F32))
                  for _ in heads)
    for h in heads:
        acc_ref[h, :, :t] = jnp.zeros((MLA_V_ROWS, t), F32)
        s_refs[0][h, :, :t] = scores(qi, 0, h)
    state = phase(0, state, qi, None, mask_half=0)
    state = phase(1, state, 0, qi, mask_half=1)

    def key_tile(jj, state):
        state = phase(0, state, jj, jnp.where(jj == 0, qi, jj - 1))
        return phase(1, state, jj + 1, jj)

    state = lax.fori_loop(0, qi // 2, lambda kk, st: key_tile(2 * kk + 1, key_tile(2 * kk, st)), state)
    state = lax.cond(qi % 2 == 1, lambda st: key_tile(qi - 1, st), lambda st: st, state)
    last_tile = jnp.maximum(qi - 1, 0)
    for h in heads:
        alpha_prev = state[h][1]
        acc = alpha_prev * acc_ref[h, :, :t] + weighted_values(last_tile, 1, h, 1)
        o_ref[0, :, h * MLA_V:(h + 1) * MLA_V] = (acc[:MLA_V] / acc[MLA_V:MLA_V + 1]).T


def _mla_call(qt, ka, vt):
    b, nq, _, t = qt.shape
    s = ka.shape[1]
    qw = MLA_STREAMS * MLA_QK_PAD
    vw = MLA_STREAMS * MLA_V_ROWS
    ow = MLA_STREAMS * MLA_V
    return pl.pallas_call(
        _mla_kernel,
        grid=(b, MLA_HEADS // MLA_STREAMS, nq),
        in_specs=[pl.BlockSpec((1, 1, qw, t), lambda bi, h, i: (bi, i, h, 0)),
                  pl.BlockSpec((1, t, qw), lambda bi, h, i: (bi, i, h)),
                  pl.BlockSpec((1, 1, vw, t), lambda bi, h, i: (bi, i, h, 0))],
        out_specs=pl.BlockSpec((1, t, ow), lambda bi, h, i: (bi, i, h)),
        out_shape=jax.ShapeDtypeStruct((b, s, MLA_WIDTH), F32),
        scratch_shapes=[pltpu.VMEM((s, qw), BF16),
                        pltpu.VMEM((nq, vw, t), BF16),
                        pltpu.VMEM((MLA_STREAMS, t // 2, t + LANES), F32),
                        pltpu.VMEM((MLA_STREAMS, t // 2, t + LANES), F32),
                        pltpu.VMEM((MLA_STREAMS, t // 2, t + LANES), BF16),
                        pltpu.VMEM((MLA_STREAMS, t // 2, t + LANES), BF16),
                        pltpu.VMEM((MLA_STREAMS, MLA_V_ROWS, t + LANES), F32)],
        compiler_params=pltpu.CompilerParams(
            dimension_semantics=("arbitrary", "arbitrary", "arbitrary"),
            vmem_limit_bytes=VMEM_LIMIT),
        name="mla_attn",
    )(qt, ka, vt)


def _swa_kernel(sink_ref, q_ref, kc_ref, kp_ref, vc_ref, vp_ref, o_ref, kf_ref):
    blk = SWA_BLOCK
    t = q_ref.shape[3]
    n_blk = t // blk
    tile = pl.program_id(1)
    kf_ref[0:blk, :] = kp_ref[0]
    kf_ref[blk:, :] = kc_ref[0]

    cols = SWA_GROUP * blk
    k_rel = lax.broadcasted_iota(jnp.int32, (2 * blk, cols), 0)
    q_rel = lax.broadcasted_iota(jnp.int32, (2 * blk, cols), 1) & (blk - 1)
    in_window = (k_rel > q_rel) & (k_rel <= q_rel + blk)
    head_of_col = lax.broadcasted_iota(jnp.int32, (1, cols), 1) // blk

    sinks = []
    for c in range(SWA_KV_HEADS):
        sink = jnp.zeros((1, cols), F32)
        for g in range(SWA_GROUP):
            sink = jnp.where(head_of_col == g, sink_ref[c * SWA_GROUP + g] * LOG2E, sink)
        sinks.append(sink)

    def band_scores(c, n):
        q_t = jnp.concatenate(
            [q_ref[0, 0, (c * SWA_GROUP + g) * LANES:(c * SWA_GROUP + g + 1) * LANES, n * blk:(n + 1) * blk]
             for g in range(SWA_GROUP)], axis=1)
        k = kf_ref[n * blk:(n + 2) * blk, c * LANES:(c + 1) * LANES]
        return _dot(k, q_t)

    def band_softmax(c, n, s):
        valid = in_window
        if n == 0:
            valid = valid & ((k_rel >= blk) | (tile > 0))
        s = jnp.where(valid, s, NEG_BIG)
        m = jnp.maximum(jnp.max(s, axis=0, keepdims=True), sinks[c])
        return jnp.exp2(s - m).astype(BF16), m

    def band_output(c, n, p, m):
        v_rows = slice(c * SWA_V_ROWS, (c + 1) * SWA_V_ROWS)
        if n == 0:
            v_t = jnp.concatenate([vp_ref[0, 0, v_rows, :], vc_ref[0, 0, v_rows, 0:blk]], axis=1)
        else:
            v_t = vc_ref[0, 0, v_rows, (n - 1) * blk:(n + 1) * blk]
        o_t = _dot(v_t, p)
        denom = o_t[SWA_HEAD_DIM:SWA_HEAD_DIM + 1] + jnp.exp2(sinks[c] - m)
        o = o_t[:SWA_HEAD_DIM] / denom
        for j in range(SWA_GROUP // 2):
            pair_t = jnp.concatenate([o[:, (2 * j) * blk:(2 * j + 1) * blk],
                                      o[:, (2 * j + 1) * blk:(2 * j + 2) * blk]], axis=0)
            col = (c * SWA_GROUP // 2 + j) * LANES
            o_ref[0, n * blk:(n + 1) * blk, col:col + LANES] = pair_t.T

    bands = [(c, n) for c in range(SWA_KV_HEADS) for n in range(n_blk)]
    score_lead, softmax_lead = 3, 1
    s_vals, p_vals = {}, {}
    for step in range(len(bands) + score_lead):
        done = step - score_lead
        if done >= 0:
            band_output(*bands[done], *p_vals.pop(done))
        if step < len(bands):
            s_vals[step] = band_scores(*bands[step])
        ready = step - (score_lead - softmax_lead)
        if 0 <= ready < len(bands):
            p_vals[ready] = band_softmax(*bands[ready], s_vals.pop(ready))


def _swa_call(qbt, kb, vbt, sinks):
    b, n_tiles, qh, t = qbt.shape
    s = kb.shape[1]
    ratio = t // SWA_BLOCK
    kw = SWA_KV_HEADS * LANES
    vh = SWA_KV_HEADS * SWA_V_ROWS
    prev_tile = lambda i: jnp.maximum(i - 1, 0)
    return pl.pallas_call(
        _swa_kernel,
        grid=(b, n_tiles),
        in_specs=[pl.BlockSpec(memory_space=pltpu.SMEM),
                  pl.BlockSpec((1, 1, qh, t), lambda bi, i: (bi, i, 0, 0)),
                  pl.BlockSpec((1, t, kw), lambda bi, i: (bi, i, 0)),
                  pl.BlockSpec((1, SWA_BLOCK, kw), lambda bi, i: (bi, jnp.maximum(i * ratio - 1, 0), 0)),
                  pl.BlockSpec((1, 1, vh, t), lambda bi, i: (bi, i, 0, 0)),
                  pl.BlockSpec((1, 1, vh, SWA_BLOCK), lambda bi, i: (bi, prev_tile(i), 0, ratio - 1))],
        out_specs=pl.BlockSpec((1, t, SWA_WIDTH), lambda bi, i: (bi, i, 0)),
        out_shape=jax.ShapeDtypeStruct((b, s, SWA_WIDTH), F32),
        scratch_shapes=[pltpu.VMEM((t + SWA_BLOCK, kw), BF16)],
        compiler_params=pltpu.CompilerParams(
            dimension_semantics=("arbitrary", "arbitrary"), vmem_limit_bytes=VMEM_LIMIT),
        name="swa_attn",
    )(sinks, qbt, kb, kb, vbt, vbt)


def _out_ffn_kernel(x_ref, oa_ref, ob_ref, ga_ref, gb_ref, wo_ref, g_ref, wg_ref, wu_ref, wd_ref, o_ref):
    na = _rms(oa_ref[...], ga_ref[0], MLA_WIDTH).astype(BF16)
    nb = _rms(ob_ref[...], gb_ref[0], SWA_WIDTH).astype(BF16)
    y = _dot(na, wo_ref[0, 0:MLA_WIDTH, :]) + _dot(nb, wo_ref[0, MLA_WIDTH:, :])
    x_new = []
    _interleave(_ffn_stages(x_ref[...] + y, g_ref[0], wg_ref, wu_ref, wd_ref, x_new))
    o_ref[...] = x_new[0]


def _out_ffn_call(x, oa, ob, layer, ga, gb, wo, ffn_consts):
    n = x.shape[0]
    tm = min(TM_ROWS, n)
    row_spec = lambda width: pl.BlockSpec((tm, width), lambda i: (i, 0))
    return pl.pallas_call(
        _out_ffn_kernel,
        grid=(n // tm,),
        in_specs=[row_spec(D_MODEL), row_spec(MLA_WIDTH), row_spec(SWA_WIDTH),
                  _layer_spec(ga.shape, layer), _layer_spec(gb.shape, layer), _layer_spec(wo.shape, layer)]
                 + [_layer_spec(c.shape, layer) for c in ffn_consts],
        out_specs=row_spec(D_MODEL),
        out_shape=jax.ShapeDtypeStruct(x.shape, F32),
        compiler_params=pltpu.CompilerParams(
            dimension_semantics=("arbitrary",), vmem_limit_bytes=VMEM_LIMIT),
        name="out_ffn",
    )(x, oa, ob, ga, gb, wo, *ffn_consts)


def _take_cols(w, cols):
    cols = np.asarray(cols)
    cuts = [0] + [i for i in range(1, len(cols)) if cols[i] != cols[i - 1] + 1] + [len(cols)]
    runs = [w[..., int(cols[a]):int(cols[b - 1]) + 1] for a, b in zip(cuts[:-1], cuts[1:])]
    return runs[0] if len(runs) == 1 else jnp.concatenate(runs, axis=-1)


def _dup_cols(n_heads, head_dim, base=0):
    half = head_dim // 2
    cols = []
    for hd in range(n_heads):
        for part in range(2):
            start = base + hd * head_dim + part * half
            cols.extend(list(range(start, start + half)) * 2)
    return np.asarray(cols, np.int32)


def _proj_consts(mix_norm, w_in, mla_q_a_norm, mla_w_q_b, mla_kv_a_norm, mla_w_kv_b,
                 mla_q_norm, mla_k_norm, swa_q_norm, swa_k_norm):
    o_kpe = MLA_Q_RANK + MLA_KV_RANK
    o_qs = o_kpe + MLA_ROPE
    o_ks = o_qs + SWA_WIDTH
    o_vs = o_ks + SWA_KV_HEADS * SWA_HEAD_DIM
    in_cols = np.concatenate([
        np.arange(0, o_kpe, dtype=np.int32),
        _dup_cols(1, MLA_ROPE, o_kpe),
        _dup_cols(SWA_KV_HEADS, SWA_HEAD_DIM, o_ks)])
    ft_cols = np.concatenate([
        np.arange(o_qs, o_ks, dtype=np.int32),
        np.arange(o_vs, o_vs + SWA_KV_HEADS * SWA_HEAD_DIM, dtype=np.int32)])
    k_cols = np.concatenate(
        [np.arange(h * (MLA_NOPE + MLA_V), h * (MLA_NOPE + MLA_V) + MLA_NOPE) for h in range(MLA_HEADS)])
    v_cols = np.concatenate(
        [np.arange(h * (MLA_NOPE + MLA_V) + MLA_NOPE, (h + 1) * (MLA_NOPE + MLA_V)) for h in range(MLA_HEADS)])
    rope_gain = _dup_cols(1, MLA_ROPE, MLA_NOPE)
    swa_gain = _dup_cols(1, SWA_HEAD_DIM)
    t = lambda w: jnp.swapaxes(w, 1, 2)
    lane_rep = lambda v: jnp.broadcast_to(v.astype(F32)[:, :, None], v.shape + (LANES,))
    return (
        _rows(mix_norm),
        _take_cols(w_in, in_cols).astype(BF16),
        t(_take_cols(w_in, ft_cols)).astype(BF16),
        _rows(mla_q_a_norm),
        t(mla_w_q_b).astype(BF16),
        _rows(mla_kv_a_norm),
        _take_cols(mla_w_kv_b, k_cols).astype(BF16),
        t(_take_cols(mla_w_kv_b, v_cols)).astype(BF16),
        lane_rep(mla_q_norm),
        _rows(mla_k_norm[:, :MLA_NOPE]),
        _rows(_take_cols(mla_k_norm, rope_gain)),
        lane_rep(swa_q_norm),
        _rows(_take_cols(swa_k_norm, swa_gain)),
    )


def _rows(v):
    return v[:, None, :].astype(F32)


def _rope_tables(seq):
    pos = jnp.arange(seq, dtype=F32)
    inv = 1.0 / (ROPE_THETA ** (jnp.arange(0, 2 * HALF, 2, dtype=F32) / (2 * HALF)))
    ang = pos[:, None] * inv[None, :]
    c, s = jnp.cos(ang), jnp.sin(ang)
    return (jnp.concatenate([c, c, c, c], axis=1), jnp.concatenate([-s, -s, s, s], axis=1), c.T, s.T)


def kernel(x, ffn1_norm, ffn1_w_gate, ffn1_w_up, ffn1_w_down, mix_norm, w_in, mla_q_a_norm, mla_w_q_b, mla_kv_a_norm, mla_w_kv_b, mla_q_norm, mla_k_norm, swa_q_norm, swa_k_norm, swa_sinks, mla_out_norm, swa_out_norm, w_o, ffn2_norm, ffn2_w_gate, ffn2_w_up, ffn2_w_down):
    b, s, d = x.shape
    depth = w_in.shape[0]
    tables = _rope_tables(s)
    ffn1 = (_rows(ffn1_norm), ffn1_w_gate.astype(BF16), ffn1_w_up.astype(BF16), ffn1_w_down.astype(BF16))
    ffn2 = (_rows(ffn2_norm), ffn2_w_gate.astype(BF16), ffn2_w_up.astype(BF16), ffn2_w_down.astype(BF16))
    consts = _proj_consts(mix_norm, w_in, mla_q_a_norm, mla_w_q_b, mla_kv_a_norm, mla_w_kv_b,
                          mla_q_norm, mla_k_norm, swa_q_norm, swa_k_norm)
    out_consts = (_rows(mla_out_norm), _rows(swa_out_norm), w_o.astype(BF16))
    xf = x.reshape(b * s, d)
    for l in range(depth):
        xf, qt, ka, vt, qbt, kb, vbt = _ffn_proj_call(xf, s, l, ffn1, consts, tables)
        shp = lambda t: t.reshape(b, s, t.shape[-1])
        tiles = lambda t: t.reshape(b, -1, t.shape[-2], t.shape[-1])
        oa = _mla_call(tiles(qt), shp(ka), tiles(vt))
        ob = _swa_call(tiles(qbt), shp(kb), tiles(vbt), swa_sinks[l].astype(F32))
        xf = _out_ffn_call(xf, oa.reshape(b * s, MLA_WIDTH), ob.reshape(b * s, SWA_WIDTH), l,
                           *out_consts, ffn2)
    return xf.reshape(b, s, d)
```

```python
import functools

import numpy as np
import jax
import jax.numpy as jnp
from jax import lax
from jax.experimental import pallas as pl
from jax.experimental.pallas import tpu as pltpu

D_MODEL = 1024
EPS = 1e-6
ROPE_THETA = 10000.0
MLA_HEADS = 4
MLA_Q_RANK = 256
MLA_KV_RANK = 128
MLA_NOPE = 128
MLA_ROPE = 64
MLA_V = 128
MLA_QK = MLA_NOPE + MLA_ROPE
MLA_WIDTH = MLA_HEADS * MLA_V
SWA_HEADS = 8
SWA_KV_HEADS = 2
SWA_GROUP = SWA_HEADS // SWA_KV_HEADS
SWA_HEAD_DIM = 64
SWA_BLOCK = 128
SWA_WIDTH = SWA_HEADS * SWA_HEAD_DIM
D_FF = 2816

LANES = 128
HALF = SWA_HEAD_DIM // 2
MLA_QK_PAD = 2 * LANES
BF16_ROWS = 16
MLA_V_ROWS = MLA_V + BF16_ROWS
NEG_BIG = -1e30

VMEM_LIMIT = 56 * 1024 * 1024

LOG2E = 1.4426950408889634

TM_ROWS = 512
SWA_TQ = 512

BF16 = jnp.bfloat16
F32 = jnp.float32


def _rms(t, gain, width):
    ss = jnp.sum(t * t, axis=-1, keepdims=True)
    return t * lax.rsqrt(ss * (1.0 / width) + EPS) * gain


def _dot(a, b):
    return jnp.dot(a, b, preferred_element_type=F32)


def _dot_nt(a, b):
    return lax.dot_general(a, b, (((1,), (1,)), ((), ())), preferred_element_type=F32)


def _ffn_stages(x, g, wg_ref, wu_ref, wd_ref, out):
    h = _rms(x, g, D_MODEL).astype(BF16)
    gate = _dot(h, wg_ref[0])
    up = _dot(h, wu_ref[0])
    yield
    half_gate = 0.5 * gate
    gated = half_gate * up
    act = (gated + gated * jnp.tanh(half_gate)).astype(BF16)
    out.append(x + 0.5 * _dot(act, wd_ref[0]))


def _interleave(*staged):
    active = list(staged)
    while active:
        for gen in list(active):
            try:
                next(gen)
            except StopIteration:
                active.remove(gen)


def _layer_spec(stacked_shape, layer):
    nd = len(stacked_shape)
    return pl.BlockSpec((1,) + tuple(stacked_shape[1:]), lambda *_: (layer,) + (0,) * (nd - 1),
                        pipeline_mode=pl.Buffered(1))


C_CQ = 0
C_CKV = C_CQ + MLA_Q_RANK
C_KPE = C_CKV + MLA_KV_RANK
C_KS = C_KPE + LANES
C_END = C_KS + SWA_KV_HEADS * LANES
R_VS = SWA_HEADS * SWA_HEAD_DIM
R_END = R_VS + SWA_KV_HEADS * SWA_HEAD_DIM
SWA_V_ROWS = SWA_HEAD_DIM + BF16_ROWS


def _rope(t, cos, sin_signed):
    return t * cos + pltpu.roll(t, 2 * HALF, 1) * sin_signed


def _proj_stages(x, g_ref, win_ref, wft_ref, gqa_ref, wqbt_ref, gkva_ref, wkb_ref, wvt_ref,
                 gq_ref, gkn_ref, gkr_ref, gsq_ref, gsk_ref, cos_ref, sin_ref, cost_ref, sint_ref,
                 qt_ref, ka_ref, vt_ref, qbt_ref, kb_ref, vbt_ref):
    tm = x.shape[0]
    q_scale = MLA_QK ** -0.5 * LOG2E
    swa_scale = SWA_HEAD_DIM ** -0.5 * LOG2E

    h = _rms(x, g_ref[0], D_MODEL).astype(BF16)
    proj = _dot(h, win_ref[0])
    proj_t = _dot_nt(wft_ref[0], h)
    yield

    cq_t = _rms(proj[:, C_CQ:C_CKV], gqa_ref[0], MLA_Q_RANK).T.astype(BF16)
    ckv = _rms(proj[:, C_CKV:C_KPE], gkva_ref[0], MLA_KV_RANK)
    q_up = _dot(wqbt_ref[0], cq_t)
    k_up = _dot(ckv.astype(BF16), wkb_ref[0])
    v_t = _dot(wvt_ref[0], ckv.T.astype(BF16)).astype(BF16)
    yield

    cos = cos_ref[...]
    sin = sin_ref[...]
    cos_t = cost_ref[...]
    sin_t = sint_ref[...]
    zeros = jnp.zeros((HALF, tm), BF16)
    ones_row = (lax.broadcasted_iota(jnp.int32, (BF16_ROWS, tm), 0) == 0).astype(BF16)

    gq = jnp.tile(gq_ref[0], (1, tm // LANES))
    for hd in range(MLA_HEADS):
        blk = q_up[hd * MLA_QK:(hd + 1) * MLA_QK]
        ss = jnp.sum(blk * blk, axis=0, keepdims=True)
        qn = blk * (lax.rsqrt(ss * (1.0 / MLA_QK) + EPS) * q_scale) * gq
        t1 = qn[MLA_NOPE:MLA_NOPE + HALF]
        t2 = qn[MLA_NOPE + HALF:]
        base = hd * MLA_QK_PAD
        qt_ref[0, base:base + MLA_NOPE, :] = qn[:MLA_NOPE].astype(BF16)
        qt_ref[0, base + MLA_NOPE:base + MLA_NOPE + HALF, :] = (t1 * cos_t - t2 * sin_t).astype(BF16)
        qt_ref[0, base + MLA_NOPE + HALF:base + MLA_NOPE + 2 * HALF, :] = zeros
        qt_ref[0, base + MLA_NOPE + 2 * HALF:base + MLA_NOPE + 3 * HALF, :] = (
            t2 * cos_t + t1 * sin_t).astype(BF16)
        qt_ref[0, base + MLA_NOPE + 3 * HALF:base + MLA_QK_PAD, :] = zeros

    for hd in range(MLA_HEADS):
        vt_ref[0, hd * MLA_V_ROWS:hd * MLA_V_ROWS + MLA_V, :] = v_t[hd * MLA_V:(hd + 1) * MLA_V]
        vt_ref[0, hd * MLA_V_ROWS + MLA_V:(hd + 1) * MLA_V_ROWS, :] = ones_row
    kpe = proj[:, C_KPE:C_KS]
    ss_pe = 0.5 * jnp.sum(kpe * kpe, axis=-1, keepdims=True)
    kpe_rot = _rope(kpe * gkr_ref[0], cos, sin)
    for hd in range(MLA_HEADS):
        nope = k_up[:, hd * MLA_NOPE:(hd + 1) * MLA_NOPE]
        ss = jnp.sum(nope * nope, axis=-1, keepdims=True) + ss_pe
        r = lax.rsqrt(ss * (1.0 / MLA_QK) + EPS)
        ka_ref[:, hd * MLA_QK_PAD: hd * MLA_QK_PAD + MLA_NOPE] = (
            nope * r * gkn_ref[0]).astype(BF16)
        ka_ref[:, hd * MLA_QK_PAD + MLA_NOPE:(hd + 1) * MLA_QK_PAD] = (kpe_rot * r).astype(BF16)

    gsq = jnp.tile(gsq_ref[0], (1, tm // LANES))
    for hd in range(SWA_HEADS):
        blk = proj_t[hd * SWA_HEAD_DIM:(hd + 1) * SWA_HEAD_DIM]
        ss = jnp.sum(blk * blk, axis=0, keepdims=True)
        qn = blk * (lax.rsqrt(ss * (1.0 / SWA_HEAD_DIM) + EPS) * swa_scale) * gsq
        t1 = qn[:HALF]
        t2 = qn[HALF:]
        base = hd * LANES
        qbt_ref[0, base:base + HALF, :] = (t1 * cos_t - t2 * sin_t).astype(BF16)
        qbt_ref[0, base + HALF:base + 2 * HALF, :] = zeros
        qbt_ref[0, base + 2 * HALF:base + 3 * HALF, :] = (t2 * cos_t + t1 * sin_t).astype(BF16)
        qbt_ref[0, base + 3 * HALF:base + LANES, :] = zeros
    for c in range(SWA_KV_HEADS):
        vbt_ref[0, c * SWA_V_ROWS:c * SWA_V_ROWS + SWA_HEAD_DIM, :] = (
            proj_t[R_VS + c * SWA_HEAD_DIM:R_VS + (c + 1) * SWA_HEAD_DIM].astype(BF16))
        vbt_ref[0, c * SWA_V_ROWS + SWA_HEAD_DIM:(c + 1) * SWA_V_ROWS, :] = ones_row
    for c in range(SWA_KV_HEADS):
        xk = proj[:, C_KS + c * LANES: C_KS + (c + 1) * LANES]
        ss = 0.5 * jnp.sum(xk * xk, axis=-1, keepdims=True)
        r = lax.rsqrt(ss * (1.0 / SWA_HEAD_DIM) + EPS)
        kb_ref[:, c * LANES:(c + 1) * LANES] = _rope(xk * r * gsk_ref[0], cos, sin).astype(BF16)


N_FFN_CONSTS = 4
N_PROJ_CONSTS = 13
N_PROJ_TABLES = 4


def _ffn_proj_kernel(*refs):
    x_ref = refs[0]
    ffn_refs = refs[1:1 + N_FFN_CONSTS]
    proj_refs = refs[1 + N_FFN_CONSTS:1 + N_FFN_CONSTS + N_PROJ_CONSTS + N_PROJ_TABLES]
    o_ref = refs[1 + N_FFN_CONSTS + N_PROJ_CONSTS + N_PROJ_TABLES]
    proj_out_refs = refs[-7:-1]
    x_prev = refs[-1]

    @pl.when(pl.program_id(0) == 0)
    def _():
        x_prev[...] = jnp.zeros_like(x_prev)

    g_ref, wg_ref, wu_ref, wd_ref = ffn_refs
    x_new = []
    _interleave(_proj_stages(x_prev[...], *proj_refs, *proj_out_refs),
                _ffn_stages(x_ref[...], g_ref[0], wg_ref, wu_ref, wd_ref, x_new))
    o_ref[...] = x_new[0]
    x_prev[...] = x_new[0]


def _ffn_proj_call(x, seq, layer, ffn_consts, consts, tables):
    n = x.shape[0]
    tm = min(TM_ROWS, seq)
    n_tiles = n // tm
    n_seq_tiles = seq // tm
    ffn_tile = lambda i: jnp.minimum(i, n_tiles - 1)
    proj_tile = lambda i: jnp.maximum(i - 1, 0)
    proj_row_spec = lambda width: pl.BlockSpec((tm, width), lambda i: (proj_tile(i), 0))
    ffn_row_spec = pl.BlockSpec((tm, D_MODEL), lambda i: (ffn_tile(i), 0))
    pos_spec = pl.BlockSpec((tm, LANES), lambda i: (proj_tile(i) % n_seq_tiles, 0))
    pos_t_spec = pl.BlockSpec((HALF, tm), lambda i: (0, proj_tile(i) % n_seq_tiles))
    row_widths = {1: MLA_HEADS * MLA_QK_PAD, 4: SWA_KV_HEADS * LANES}
    col_heights = {0: MLA_HEADS * MLA_QK_PAD, 2: MLA_HEADS * MLA_V_ROWS,
                   3: SWA_HEADS * LANES, 5: SWA_KV_HEADS * SWA_V_ROWS}
    out_specs, out_shape = [ffn_row_spec], [jax.ShapeDtypeStruct(x.shape, F32)]
    for idx in range(6):
        if idx in row_widths:
            out_specs.append(proj_row_spec(row_widths[idx]))
            out_shape.append(jax.ShapeDtypeStruct((n, row_widths[idx]), BF16))
        else:
            out_specs.append(pl.BlockSpec((1, col_heights[idx], tm), lambda i: (proj_tile(i), 0, 0)))
            out_shape.append(jax.ShapeDtypeStruct((n_tiles, col_heights[idx], tm), BF16))
    assert len(ffn_consts) == N_FFN_CONSTS and len(consts) == N_PROJ_CONSTS and len(tables) == N_PROJ_TABLES
    return pl.pallas_call(
        _ffn_proj_kernel,
        grid=(n_tiles + 1,),
        in_specs=([ffn_row_spec] + [_layer_spec(c.shape, layer) for c in ffn_consts]
                  + [_layer_spec(c.shape, layer) for c in consts]
                  + [pos_spec, pos_spec, pos_t_spec, pos_t_spec]),
        out_specs=out_specs,
        out_shape=out_shape,
        scratch_shapes=[pltpu.VMEM((tm, D_MODEL), F32)],
        compiler_params=pltpu.CompilerParams(
            dimension_semantics=("arbitrary",), vmem_limit_bytes=VMEM_LIMIT),
        name="ffn_proj",
    )(x, *ffn_consts, *consts, *tables)


MLA_UNROLL = 4
MLA_STREAMS = 4


def _mla_kernel(q_ref, k_tile_ref, v_tile_ref, o_ref, k_ref, v_ref, s0_ref, s1_ref, p0_ref, p1_ref, acc_ref):
    t = q_ref.shape[3]
    tk = t // 2
    qi = pl.program_id(2)
    heads = range(MLA_STREAMS)
    s_refs = (s0_ref, s1_ref)
    p_refs = (p0_ref, p1_ref)

    k_ref[pl.ds(pl.multiple_of(qi * t, t), t), :] = k_tile_ref[0]
    v_ref[qi] = v_tile_ref[0, 0]

    def scores(tile, half, h):
        start = pl.multiple_of(tile * t, t) + half * tk
        k = k_ref[pl.ds(start, tk), h * MLA_QK_PAD:(h + 1) * MLA_QK_PAD]
        return _dot(k, q_ref[0, 0, h * MLA_QK_PAD:(h + 1) * MLA_QK_PAD, :])

    def weighted_values(tile, half, h, buf):
        v = v_ref[tile, h * MLA_V_ROWS:(h + 1) * MLA_V_ROWS, half * tk:(half + 1) * tk]
        return _dot(v, p_refs[buf][h, :, :t])

    def phase(cur, state, nxt_tile, prev_tile, mask_half=None):
        new_state = []
        for h in heads:
            m, alpha_prev, block_max = state[h]
            if prev_tile is not None:
                pv = weighted_values(prev_tile, 1 - cur, h, 1 - cur)
            next_max = block_max
            if nxt_tile is not None:
                nxt = scores(nxt_tile, 1 - cur, h)
                s_refs[1 - cur][h, :, :t] = nxt
                next_max = jnp.max(nxt, axis=0, keepdims=True)
            s = s_refs[cur][h, :, :t]
            if mask_half is not None:
                key = lax.broadcasted_iota(jnp.int32, (tk, t), 0) + mask_half * tk
                qry = lax.broadcasted_iota(jnp.int32, (tk, t), 1)
                s = jnp.where(key <= qry, s, NEG_BIG)
                block_max = jnp.max(s, axis=0, keepdims=True)
            m_new = jnp.maximum(m, block_max)
            p_refs[cur][h, :, :t] = jnp.exp2(s - m_new).astype(BF16)
            alpha = jnp.exp2(m - m_new)
            if prev_tile is not None:
                acc_ref[h, :, :t] = alpha_prev * acc_ref[h, :, :t] + pv
            new_state.append((m_new, alpha, next_max))
        return tuple(new_state)

    state = tuple((jnp.full((1, t), NEG_BIG, F32), jnp.zeros((1, t), F32), jnp.zeros((1, t), F32))
                  for _ in heads)
    for h in heads:
        acc_ref[h, :, :t] = jnp.zeros((MLA_V_ROWS, t), F32)
        s_refs[0][h, :, :t] = scores(qi, 0, h)
    state = phase(0, state, qi, None, mask_half=0)
    state = phase(1, state, 0, qi, mask_half=1)

    def key_tile(jj, state):
        state = phase(0, state, jj, jnp.where(jj == 0, qi, jj - 1))
        return phase(1, state, jj + 1, jj)

    def key_tiles(first, count, state):
        for j in range(count):
            state = key_tile(first + j, state)
        return state

    group = MLA_UNROLL
    state = lax.fori_loop(0, qi // group, lambda kk, st: key_tiles(group * kk, group, st), state)
    done = (qi // group) * group
    while group > 1:
        group //= 2
        take = ((qi - done) // group) * group
        state = lax.cond(take > 0, functools.partial(key_tiles, done, group), lambda st: st, state)
        done = done + take
    last_tile = jnp.maximum(qi - 1, 0)
    for h in heads:
        alpha_prev = state[h][1]
        acc = alpha_prev * acc_ref[h, :, :t] + weighted_values(last_tile, 1, h, 1)
        o_ref[0, :, h * MLA_V:(h + 1) * MLA_V] = (acc[:MLA_V] / acc[MLA_V:MLA_V + 1]).T


def _mla_call(qt, ka, vt):
    b, nq, _, t = qt.shape
    s = ka.shape[1]
    qw = MLA_STREAMS * MLA_QK_PAD
    vw = MLA_STREAMS * MLA_V_ROWS
    ow = MLA_STREAMS * MLA_V
    return pl.pallas_call(
        _mla_kernel,
        grid=(b, MLA_HEADS // MLA_STREAMS, nq),
        in_specs=[pl.BlockSpec((1, 1, qw, t), lambda bi, h, i: (bi, i, h, 0)),
                  pl.BlockSpec((1, t, qw), lambda bi, h, i: (bi, i, h)),
                  pl.BlockSpec((1, 1, vw, t), lambda bi, h, i: (bi, i, h, 0))],
        out_specs=pl.BlockSpec((1, t, ow), lambda bi, h, i: (bi, i, h)),
        out_shape=jax.ShapeDtypeStruct((b, s, MLA_WIDTH), F32),
        scratch_shapes=[pltpu.VMEM((s, qw), BF16),
                        pltpu.VMEM((nq, vw, t), BF16),
                        pltpu.VMEM((MLA_STREAMS, t // 2, t + LANES), F32),
                        pltpu.VMEM((MLA_STREAMS, t // 2, t + LANES), F32),
                        pltpu.VMEM((MLA_STREAMS, t // 2, t + LANES), BF16),
                        pltpu.VMEM((MLA_STREAMS, t // 2, t + LANES), BF16),
                        pltpu.VMEM((MLA_STREAMS, MLA_V_ROWS, t + LANES), F32)],
        compiler_params=pltpu.CompilerParams(
            dimension_semantics=("arbitrary", "arbitrary", "arbitrary"),
            vmem_limit_bytes=VMEM_LIMIT),
        name="mla_attn",
    )(qt, ka, vt)


def _swa_kernel(sink_ref, q_ref, kc_ref, kp_ref, vc_ref, vp_ref, o_ref, kf_ref):
    blk = SWA_BLOCK
    t = q_ref.shape[3]
    n_blk = t // blk
    tile = pl.program_id(1)
    kf_ref[0:blk, :] = kp_ref[0]
    kf_ref[blk:, :] = kc_ref[0]

    cols = SWA_GROUP * blk
    k_rel = lax.broadcasted_iota(jnp.int32, (2 * blk, cols), 0)
    q_rel = lax.broadcasted_iota(jnp.int32, (2 * blk, cols), 1) & (blk - 1)
    in_window = (k_rel > q_rel) & (k_rel <= q_rel + blk)
    head_of_col = lax.broadcasted_iota(jnp.int32, (1, cols), 1) // blk

    sinks = []
    for c in range(SWA_KV_HEADS):
        sink = jnp.zeros((1, cols), F32)
        for g in range(SWA_GROUP):
            sink = jnp.where(head_of_col == g, sink_ref[c * SWA_GROUP + g] * LOG2E, sink)
        sinks.append(sink)

    def band_scores(c, n):
        q_t = jnp.concatenate(
            [q_ref[0, 0, (c * SWA_GROUP + g) * LANES:(c * SWA_GROUP + g + 1) * LANES, n * blk:(n + 1) * blk]
             for g in range(SWA_GROUP)], axis=1)
        k = kf_ref[n * blk:(n + 2) * blk, c * LANES:(c + 1) * LANES]
        return _dot(k, q_t)

    def band_softmax(c, n, s):
        valid = in_window
        if n == 0:
            valid = valid & ((k_rel >= blk) | (tile > 0))
        s = jnp.where(valid, s, NEG_BIG)
        m = jnp.maximum(jnp.max(s, axis=0, keepdims=True), sinks[c])
        return jnp.exp2(s - m).astype(BF16), m

    def band_output(c, n, p, m):
        v_rows = slice(c * SWA_V_ROWS, (c + 1) * SWA_V_ROWS)
        if n == 0:
            v_t = jnp.concatenate([vp_ref[0, 0, v_rows, :], vc_ref[0, 0, v_rows, 0:blk]], axis=1)
        else:
            v_t = vc_ref[0, 0, v_rows, (n - 1) * blk:(n + 1) * blk]
        o_t = _dot(v_t, p)
        denom = o_t[SWA_HEAD_DIM:SWA_HEAD_DIM + 1] + jnp.exp2(sinks[c] - m)
        o = o_t[:SWA_HEAD_DIM] / denom
        for j in range(SWA_GROUP // 2):
            pair_t = jnp.concatenate([o[:, (2 * j) * blk:(2 * j + 1) * blk],
                                      o[:, (2 * j + 1) * blk:(2 * j + 2) * blk]], axis=0)
            col = (c * SWA_GROUP // 2 + j) * LANES
            o_ref[0, n * blk:(n + 1) * blk, col:col + LANES] = pair_t.T

    bands = [(c, n) for c in range(SWA_KV_HEADS) for n in range(n_blk)]
    score_lead, softmax_lead = 3, 1
    s_vals, p_vals = {}, {}
    for step in range(len(bands) + score_lead):
        done = step - score_lead
        if done >= 0:
            band_output(*bands[done], *p_vals.pop(done))
        if step < len(bands):
            s_vals[step] = band_scores(*bands[step])
        ready = step - (score_lead - softmax_lead)
        if 0 <= ready < len(bands):
            p_vals[ready] = band_softmax(*bands[ready], s_vals.pop(ready))


def _swa_call(qbt, kb, vbt, sinks):
    b, n_tiles, qh, t = qbt.shape
    s = kb.shape[1]
    ratio = t // SWA_BLOCK
    kw = SWA_KV_HEADS * LANES
    vh = SWA_KV_HEADS * SWA_V_ROWS
    prev_tile = lambda i: jnp.maximum(i - 1, 0)
    return pl.pallas_call(
        _swa_kernel,
        grid=(b, n_tiles),
        in_specs=[pl.BlockSpec(memory_space=pltpu.SMEM),
                  pl.BlockSpec((1, 1, qh, t), lambda bi, i: (bi, i, 0, 0)),
                  pl.BlockSpec((1, t, kw), lambda bi, i: (bi, i, 0)),
                  pl.BlockSpec((1, SWA_BLOCK, kw), lambda bi, i: (bi, jnp.maximum(i * ratio - 1, 0), 0)),
                  pl.BlockSpec((1, 1, vh, t), lambda bi, i: (bi, i, 0, 0)),
                  pl.BlockSpec((1, 1, vh, SWA_BLOCK), lambda bi, i: (bi, prev_tile(i), 0, ratio - 1))],
        out_specs=pl.BlockSpec((1, t, SWA_WIDTH), lambda bi, i: (bi, i, 0)),
        out_shape=jax.ShapeDtypeStruct((b, s, SWA_WIDTH), F32),
        scratch_shapes=[pltpu.VMEM((t + SWA_BLOCK, kw), BF16)],
        compiler_params=pltpu.CompilerParams(
            dimension_semantics=("arbitrary", "arbitrary"), vmem_limit_bytes=VMEM_LIMIT),
        name="swa_attn",
    )(sinks, qbt, kb, kb, vbt, vbt)


def _out_ffn_kernel(x_ref, oa_ref, ob_ref, ga_ref, gb_ref, wo_ref, g_ref, wg_ref, wu_ref, wd_ref, o_ref):
    na = _rms(oa_ref[...], ga_ref[0], MLA_WIDTH).astype(BF16)
    nb = _rms(ob_ref[...], gb_ref[0], SWA_WIDTH).astype(BF16)
    y = _dot(na, wo_ref[0, 0:MLA_WIDTH, :]) + _dot(nb, wo_ref[0, MLA_WIDTH:, :])
    x_new = []
    _interleave(_ffn_stages(x_ref[...] + y, g_ref[0], wg_ref, wu_ref, wd_ref, x_new))
    o_ref[...] = x_new[0]


def _out_ffn_call(x, oa, ob, layer, ga, gb, wo, ffn_consts):
    n = x.shape[0]
    tm = min(TM_ROWS, n)
    row_spec = lambda width: pl.BlockSpec((tm, width), lambda i: (i, 0))
    return pl.pallas_call(
        _out_ffn_kernel,
        grid=(n // tm,),
        in_specs=[row_spec(D_MODEL), row_spec(MLA_WIDTH), row_spec(SWA_WIDTH),
                  _layer_spec(ga.shape, layer), _layer_spec(gb.shape, layer), _layer_spec(wo.shape, layer)]
                 + [_layer_spec(c.shape, layer) for c in ffn_consts],
        out_specs=row_spec(D_MODEL),
        out_shape=jax.ShapeDtypeStruct(x.shape, F32),
        compiler_params=pltpu.CompilerParams(
            dimension_semantics=("arbitrary",), vmem_limit_bytes=VMEM_LIMIT),
        name="out_ffn",
    )(x, oa, ob, ga, gb, wo, *ffn_consts)


def _take_cols(w, cols):
    cols = np.asarray(cols)
    cuts = [0] + [i for i in range(1, len(cols)) if cols[i] != cols[i - 1] + 1] + [len(cols)]
    runs = [w[..., int(cols[a]):int(cols[b - 1]) + 1] for a, b in zip(cuts[:-1], cuts[1:])]
    return runs[0] if len(runs) == 1 else jnp.concatenate(runs, axis=-1)


def _dup_cols(n_heads, head_dim, base=0):
    half = head_dim // 2
    cols = []
    for hd in range(n_heads):
        for part in range(2):
            start = base + hd * head_dim + part * half
            cols.extend(list(range(start, start + half)) * 2)
    return np.asarray(cols, np.int32)


def _proj_consts(mix_norm, w_in, mla_q_a_norm, mla_w_q_b, mla_kv_a_norm, mla_w_kv_b,
                 mla_q_norm, mla_k_norm, swa_q_norm, swa_k_norm):
    o_kpe = MLA_Q_RANK + MLA_KV_RANK
    o_qs = o_kpe + MLA_ROPE
    o_ks = o_qs + SWA_WIDTH
    o_vs = o_ks + SWA_KV_HEADS * SWA_HEAD_DIM
    in_cols = np.concatenate([
        np.arange(0, o_kpe, dtype=np.int32),
        _dup_cols(1, MLA_ROPE, o_kpe),
        _dup_cols(SWA_KV_HEADS, SWA_HEAD_DIM, o_ks)])
    ft_cols = np.concatenate([
        np.arange(o_qs, o_ks, dtype=np.int32),
        np.arange(o_vs, o_vs + SWA_KV_HEADS * SWA_HEAD_DIM, dtype=np.int32)])
    k_cols = np.concatenate(
        [np.arange(h * (MLA_NOPE + MLA_V), h * (MLA_NOPE + MLA_V) + MLA_NOPE) for h in range(MLA_HEADS)])
    v_cols = np.concatenate(
        [np.arange(h * (MLA_NOPE + MLA_V) + MLA_NOPE, (h + 1) * (MLA_NOPE + MLA_V)) for h in range(MLA_HEADS)])
    rope_gain = _dup_cols(1, MLA_ROPE, MLA_NOPE)
    swa_gain = _dup_cols(1, SWA_HEAD_DIM)
    t = lambda w: jnp.swapaxes(w, 1, 2)
    lane_rep = lambda v: jnp.broadcast_to(v.astype(F32)[:, :, None], v.shape + (LANES,))
    return (
        _rows(mix_norm),
        _take_cols(w_in, in_cols).astype(BF16),
        t(_take_cols(w_in, ft_cols)).astype(BF16),
        _rows(mla_q_a_norm),
        t(mla_w_q_b).astype(BF16),
        _rows(mla_kv_a_norm),
        _take_cols(mla_w_kv_b, k_cols).astype(BF16),
        t(_take_cols(mla_w_kv_b, v_cols)).astype(BF16),
        lane_rep(mla_q_norm),
        _rows(mla_k_norm[:, :MLA_NOPE]),
        _rows(_take_cols(mla_k_norm, rope_gain)),
        lane_rep(swa_q_norm),
        _rows(_take_cols(swa_k_norm, swa_gain)),
    )


def _rows(v):
    return v[:, None, :].astype(F32)


def _rope_tables(seq):
    pos = jnp.arange(seq, dtype=F32)
    inv = 1.0 / (ROPE_THETA ** (jnp.arange(0, 2 * HALF, 2, dtype=F32) / (2 * HALF)))
    ang = pos[:, None] * inv[None, :]
    c, s = jnp.cos(ang), jnp.sin(ang)
    return (jnp.concatenate([c, c, c, c], axis=1), jnp.concatenate([-s, -s, s, s], axis=1), c.T, s.T)


def kernel(x, ffn1_norm, ffn1_w_gate, ffn1_w_up, ffn1_w_down, mix_norm, w_in, mla_q_a_norm, mla_w_q_b, mla_kv_a_norm, mla_w_kv_b, mla_q_norm, mla_k_norm, swa_q_norm, swa_k_norm, swa_sinks, mla_out_norm, swa_out_norm, w_o, ffn2_norm, ffn2_w_gate, ffn2_w_up, ffn2_w_down):
    b, s, d = x.shape
    depth = w_in.shape[0]
    tables = _rope_tables(s)
    ffn1 = (_rows(ffn1_norm), ffn1_w_gate.astype(BF16), ffn1_w_up.astype(BF16), ffn1_w_down.astype(BF16))
    ffn2 = (_rows(ffn2_norm), ffn2_w_gate.astype(BF16), ffn2_w_up.astype(BF16), ffn2_w_down.astype(BF16))
    consts = _proj_consts(mix_norm, w_in, mla_q_a_norm, mla_w_q_b, mla_kv_a_norm, mla_w_kv_b,
                          mla_q_norm, mla_k_norm, swa_q_norm, swa_k_norm)
    out_consts = (_rows(mla_out_norm), _rows(swa_out_norm), w_o.astype(BF16))
    xf = x.reshape(b * s, d)
    for l in range(depth):
        xf, qt, ka, vt, qbt, kb, vbt = _ffn_proj_call(xf, s, l, ffn1, consts, tables)
        shp = lambda t: t.reshape(b, s, t.shape[-1])
        tiles = lambda t: t.reshape(b, -1, t.shape[-2], t.shape[-1])
        oa = _mla_call(tiles(qt), shp(ka), tiles(vt))
        ob = _swa_call(tiles(qbt), shp(kb), tiles(vbt), swa_sinks[l].astype(F32))
        xf = _out_ffn_call(xf, oa.reshape(b * s, MLA_WIDTH), ob.reshape(b * s, SWA_WIDTH), l,
                           *out_consts, ffn2)
    return xf.reshape(b, s, d)
```

```python
import functools

import numpy as np
import jax
import jax.numpy as jnp
from jax import lax
from jax.experimental import pallas as pl
from jax.experimental.pallas import tpu as pltpu

D_MODEL = 1024
EPS = 1e-6
ROPE_THETA = 10000.0
MLA_HEADS = 4
MLA_Q_RANK = 256
MLA_KV_RANK = 128
MLA_NOPE = 128
MLA_ROPE = 64
MLA_V = 128
MLA_QK = MLA_NOPE + MLA_ROPE
MLA_WIDTH = MLA_HEADS * MLA_V
SWA_HEADS = 8
SWA_KV_HEADS = 2
SWA_GROUP = SWA_HEADS // SWA_KV_HEADS
SWA_HEAD_DIM = 64
SWA_BLOCK = 128
SWA_WIDTH = SWA_HEADS * SWA_HEAD_DIM
D_FF = 2816

LANES = 128
HALF = SWA_HEAD_DIM // 2
MLA_QK_PAD = 2 * LANES
BF16_ROWS = 16
MLA_V_ROWS = MLA_V + BF16_ROWS
NEG_BIG = -1e30

VMEM_LIMIT = 56 * 1024 * 1024

LOG2E = 1.4426950408889634

TM_ROWS = 512
SWA_TQ = 512

BF16 = jnp.bfloat16
F32 = jnp.float32


def _rms(t, gain, width):
    ss = jnp.sum(t * t, axis=-1, keepdims=True)
    return t * lax.rsqrt(ss * (1.0 / width) + EPS) * gain


def _dot(a, b):
    return jnp.dot(a, b, preferred_element_type=F32)


def _dot_nt(a, b):
    return lax.dot_general(a, b, (((1,), (1,)), ((), ())), preferred_element_type=F32)


def _ffn_stages(x, g, wg_ref, wu_ref, wd_ref, out):
    h = _rms(x, g, D_MODEL).astype(BF16)
    gate = _dot(h, wg_ref[0])
    up = _dot(h, wu_ref[0])
    yield
    half_gate = 0.5 * gate
    gated = half_gate * up
    act = (gated + gated * jnp.tanh(half_gate)).astype(BF16)
    out.append(x + 0.5 * _dot(act, wd_ref[0]))


def _interleave(*staged):
    active = list(staged)
    while active:
        for gen in list(active):
            try:
                next(gen)
            except StopIteration:
                active.remove(gen)


def _layer_spec(stacked_shape, layer):
    nd = len(stacked_shape)
    return pl.BlockSpec((1,) + tuple(stacked_shape[1:]), lambda *_: (layer,) + (0,) * (nd - 1),
                        pipeline_mode=pl.Buffered(1))


C_CQ = 0
C_CKV = C_CQ + MLA_Q_RANK
C_KPE = C_CKV + MLA_KV_RANK
C_KS = C_KPE + LANES
C_END = C_KS + SWA_KV_HEADS * LANES
R_VS = SWA_HEADS * SWA_HEAD_DIM
R_END = R_VS + SWA_KV_HEADS * SWA_HEAD_DIM
SWA_V_ROWS = SWA_HEAD_DIM + BF16_ROWS


def _rope(t, cos, sin_signed):
    return t * cos + pltpu.roll(t, 2 * HALF, 1) * sin_signed


def _proj_stages(x, g_ref, win_ref, wft_ref, gqa_ref, wqbt_ref, gkva_ref, wkb_ref, wvt_ref,
                 gq_ref, gkn_ref, gkr_ref, gsq_ref, gsk_ref, cos_ref, sin_ref, cost_ref, sint_ref,
                 qt_ref, ka_ref, vt_ref, qbt_ref, kb_ref, vbt_ref):
    tm = x.shape[0]
    q_scale = MLA_QK ** -0.5 * LOG2E
    swa_scale = SWA_HEAD_DIM ** -0.5 * LOG2E

    h = _rms(x, g_ref[0], D_MODEL).astype(BF16)
    proj = _dot(h, win_ref[0])
    proj_t = _dot_nt(wft_ref[0], h)
    yield

    cq_t = _rms(proj[:, C_CQ:C_CKV], gqa_ref[0], MLA_Q_RANK).T.astype(BF16)
    ckv = _rms(proj[:, C_CKV:C_KPE], gkva_ref[0], MLA_KV_RANK)
    q_up = _dot(wqbt_ref[0], cq_t)
    k_up = _dot(ckv.astype(BF16), wkb_ref[0])
    v_t = _dot(wvt_ref[0], ckv.T.astype(BF16)).astype(BF16)
    yield

    cos = cos_ref[...]
    sin = sin_ref[...]
    cos_t = cost_ref[...]
    sin_t = sint_ref[...]
    zeros = jnp.zeros((HALF, tm), BF16)
    ones_row = (lax.broadcasted_iota(jnp.int32, (BF16_ROWS, tm), 0) == 0).astype(BF16)

    gq = jnp.tile(gq_ref[0], (1, tm // LANES))
    for hd in range(MLA_HEADS):
        blk = q_up[hd * MLA_QK:(hd + 1) * MLA_QK]
        ss = jnp.sum(blk * blk, axis=0, keepdims=True)
        qn = blk * (lax.rsqrt(ss * (1.0 / MLA_QK) + EPS) * q_scale) * gq
        t1 = qn[MLA_NOPE:MLA_NOPE + HALF]
        t2 = qn[MLA_NOPE + HALF:]
        base = hd * MLA_QK_PAD
        qt_ref[0, base:base + MLA_NOPE, :] = qn[:MLA_NOPE].astype(BF16)
        qt_ref[0, base + MLA_NOPE:base + MLA_NOPE + HALF, :] = (t1 * cos_t - t2 * sin_t).astype(BF16)
        qt_ref[0, base + MLA_NOPE + HALF:base + MLA_NOPE + 2 * HALF, :] = zeros
        qt_ref[0, base + MLA_NOPE + 2 * HALF:base + MLA_NOPE + 3 * HALF, :] = (
            t2 * cos_t + t1 * sin_t).astype(BF16)
        qt_ref[0, base + MLA_NOPE + 3 * HALF:base + MLA_QK_PAD, :] = zeros

    for hd in range(MLA_HEADS):
        vt_ref[0, hd * MLA_V_ROWS:hd * MLA_V_ROWS + MLA_V, :] = v_t[hd * MLA_V:(hd + 1) * MLA_V]
        vt_ref[0, hd * MLA_V_ROWS + MLA_V:(hd + 1) * MLA_V_ROWS, :] = ones_row
    kpe = proj[:, C_KPE:C_KS]
    ss_pe = 0.5 * jnp.sum(kpe * kpe, axis=-1, keepdims=True)
    kpe_rot = _rope(kpe * gkr_ref[0], cos, sin)
    for hd in range(MLA_HEADS):
        nope = k_up[:, hd * MLA_NOPE:(hd + 1) * MLA_NOPE]
        ss = jnp.sum(nope * nope, axis=-1, keepdims=True) + ss_pe
        r = lax.rsqrt(ss * (1.0 / MLA_QK) + EPS)
        ka_ref[:, hd * MLA_QK_PAD: hd * MLA_QK_PAD + MLA_NOPE] = (
            nope * r * gkn_ref[0]).astype(BF16)
        ka_ref[:, hd * MLA_QK_PAD + MLA_NOPE:(hd + 1) * MLA_QK_PAD] = (kpe_rot * r).astype(BF16)

    gsq = jnp.tile(gsq_ref[0], (1, tm // LANES))
    for hd in range(SWA_HEADS):
        blk = proj_t[hd * SWA_HEAD_DIM:(hd + 1) * SWA_HEAD_DIM]
        ss = jnp.sum(blk * blk, axis=0, keepdims=True)
        qn = blk * (lax.rsqrt(ss * (1.0 / SWA_HEAD_DIM) + EPS) * swa_scale) * gsq
        t1 = qn[:HALF]
        t2 = qn[HALF:]
        base = hd * LANES
        qbt_ref[0, base:base + HALF, :] = (t1 * cos_t - t2 * sin_t).astype(BF16)
        qbt_ref[0, base + HALF:base + 2 * HALF, :] = zeros
        qbt_ref[0, base + 2 * HALF:base + 3 * HALF, :] = (t2 * cos_t + t1 * sin_t).astype(BF16)
        qbt_ref[0, base + 3 * HALF:base + LANES, :] = zeros
    for c in range(SWA_KV_HEADS):
        vbt_ref[0, c * SWA_V_ROWS:c * SWA_V_ROWS + SWA_HEAD_DIM, :] = (
            proj_t[R_VS + c * SWA_HEAD_DIM:R_VS + (c + 1) * SWA_HEAD_DIM].astype(BF16))
        vbt_ref[0, c * SWA_V_ROWS + SWA_HEAD_DIM:(c + 1) * SWA_V_ROWS, :] = ones_row
    for c in range(SWA_KV_HEADS):
        xk = proj[:, C_KS + c * LANES: C_KS + (c + 1) * LANES]
        ss = 0.5 * jnp.sum(xk * xk, axis=-1, keepdims=True)
        r = lax.rsqrt(ss * (1.0 / SWA_HEAD_DIM) + EPS)
        kb_ref[:, c * LANES:(c + 1) * LANES] = _rope(xk * r * gsk_ref[0], cos, sin).astype(BF16)


N_FFN_CONSTS = 4
N_PROJ_CONSTS = 13
N_PROJ_TABLES = 4


def _ffn_proj_kernel(*refs):
    x_ref = refs[0]
    ffn_refs = refs[1:1 + N_FFN_CONSTS]
    proj_refs = refs[1 + N_FFN_CONSTS:1 + N_FFN_CONSTS + N_PROJ_CONSTS + N_PROJ_TABLES]
    o_ref = refs[1 + N_FFN_CONSTS + N_PROJ_CONSTS + N_PROJ_TABLES]
    proj_out_refs = refs[-7:-1]
    x_prev = refs[-1]

    @pl.when(pl.program_id(0) == 0)
    def _():
        x_prev[...] = jnp.zeros_like(x_prev)

    g_ref, wg_ref, wu_ref, wd_ref = ffn_refs
    x_new = []
    _interleave(_proj_stages(x_prev[...], *proj_refs, *proj_out_refs),
                _ffn_stages(x_ref[...], g_ref[0], wg_ref, wu_ref, wd_ref, x_new))
    o_ref[...] = x_new[0]
    x_prev[...] = x_new[0]


def _ffn_proj_call(x, seq, layer, ffn_consts, consts, tables):
    n = x.shape[0]
    tm = min(TM_ROWS, seq)
    n_tiles = n // tm
    n_seq_tiles = seq // tm
    ffn_tile = lambda i: jnp.minimum(i, n_tiles - 1)
    proj_tile = lambda i: jnp.maximum(i - 1, 0)
    proj_row_spec = lambda width: pl.BlockSpec((tm, width), lambda i: (proj_tile(i), 0))
    ffn_row_spec = pl.BlockSpec((tm, D_MODEL), lambda i: (ffn_tile(i), 0))
    pos_spec = pl.BlockSpec((tm, LANES), lambda i: (proj_tile(i) % n_seq_tiles, 0))
    pos_t_spec = pl.BlockSpec((HALF, tm), lambda i: (0, proj_tile(i) % n_seq_tiles))
    row_widths = {1: MLA_HEADS * MLA_QK_PAD, 4: SWA_KV_HEADS * LANES}
    col_heights = {0: MLA_HEADS * MLA_QK_PAD, 2: MLA_HEADS * MLA_V_ROWS,
                   3: SWA_HEADS * LANES, 5: SWA_KV_HEADS * SWA_V_ROWS}
    out_specs, out_shape = [ffn_row_spec], [jax.ShapeDtypeStruct(x.shape, F32)]
    for idx in range(6):
        if idx in row_widths:
            out_specs.append(proj_row_spec(row_widths[idx]))
            out_shape.append(jax.ShapeDtypeStruct((n, row_widths[idx]), BF16))
        else:
            out_specs.append(pl.BlockSpec((1, col_heights[idx], tm), lambda i: (proj_tile(i), 0, 0)))
            out_shape.append(jax.ShapeDtypeStruct((n_tiles, col_heights[idx], tm), BF16))
    assert len(ffn_consts) == N_FFN_CONSTS and len(consts) == N_PROJ_CONSTS and len(tables) == N_PROJ_TABLES
    return pl.pallas_call(
        _ffn_proj_kernel,
        grid=(n_tiles + 1,),
        in_specs=([ffn_row_spec] + [_layer_spec(c.shape, layer) for c in ffn_consts]
                  + [_layer_spec(c.shape, layer) for c in consts]
                  + [pos_spec, pos_spec, pos_t_spec, pos_t_spec]),
        out_specs=out_specs,
        out_shape=out_shape,
        scratch_shapes=[pltpu.VMEM((tm, D_MODEL), F32)],
        compiler_params=pltpu.CompilerParams(
            dimension_semantics=("arbitrary",), vmem_limit_bytes=VMEM_LIMIT),
        name="ffn_proj",
    )(x, *ffn_consts, *consts, *tables)


MLA_UNROLL = 4
MLA_STREAMS = 4


def _mla_kernel(q_ref, k_tile_ref, v_tile_ref, o_ref, k_ref, v_ref, s0_ref, s1_ref, p0_ref, p1_ref, acc_ref):
    t = q_ref.shape[3]
    tk = t // 2
    qi = pl.program_id(2)
    heads = range(MLA_STREAMS)
    s_refs = (s0_ref, s1_ref)
    p_refs = (p0_ref, p1_ref)

    k_ref[pl.ds(pl.multiple_of(qi * t, t), t), :] = k_tile_ref[0]
    v_ref[qi] = v_tile_ref[0, 0]

    def scores(tile, half, h):
        start = pl.multiple_of(tile * t, t) + half * tk
        k = k_ref[pl.ds(start, tk), h * MLA_QK_PAD:(h + 1) * MLA_QK_PAD]
        return _dot(k, q_ref[0, 0, h * MLA_QK_PAD:(h + 1) * MLA_QK_PAD, :])

    def weighted_values(tile, half, h, buf):
        v = v_ref[tile, h * MLA_V_ROWS:(h + 1) * MLA_V_ROWS, half * tk:(half + 1) * tk]
        return _dot(v, p_refs[buf][h, :, :t])

    def phase(cur, state, nxt_tile, prev_tile, mask_half=None):
        new_state = []
        for h in heads:
            m, alpha_prev, block_max = state[h]
            if prev_tile is not None:
                pv = weighted_values(prev_tile, 1 - cur, h, 1 - cur)
            next_max = block_max
            if nxt_tile is not None:
                nxt = scores(nxt_tile, 1 - cur, h)
                s_refs[1 - cur][h, :, :t] = nxt
                next_max = jnp.max(nxt, axis=0, keepdims=True)
            s = s_refs[cur][h, :, :t]
            if mask_half is not None:
                key = lax.broadcasted_iota(jnp.int32, (tk, t), 0) + mask_half * tk
                qry = lax.broadcasted_iota(jnp.int32, (tk, t), 1)
                s = jnp.where(key <= qry, s, NEG_BIG)
                block_max = jnp.max(s, axis=0, keepdims=True)
            m_new = jnp.maximum(m, block_max)
            p_refs[cur][h, :, :t] = jnp.exp2(s - m_new).astype(BF16)
            alpha = jnp.exp2(m - m_new)
            if prev_tile is not None:
                acc_ref[h, :, :t] = alpha_prev * acc_ref[h, :, :t] + pv
            new_state.append((m_new, alpha, next_max))
        return tuple(new_state)

    def diagonal(state):
        for h in heads:
            acc_ref[h, :, :t] = jnp.zeros((MLA_V_ROWS, t), F32)
            s_refs[0][h, :, :t] = scores(qi, 0, h)
        state = phase(0, state, qi, None, mask_half=0)
        return phase(1, state, 0, qi, mask_half=1)

    def key_tile(jj, state):
        state = phase(0, state, jj, jnp.where(jj == 0, qi, jj - 1))
        return phase(1, state, jj + 1, jj)

    def key_tiles(first, count, state):
        for j in range(count):
            state = key_tile(first + j, state)
        return state

    def finish(state):
        last_tile = jnp.maximum(qi - 1, 0)
        for h in heads:
            alpha_prev = state[h][1]
            acc = alpha_prev * acc_ref[h, :, :t] + weighted_values(last_tile, 1, h, 1)
            o_ref[0, :, h * MLA_V:(h + 1) * MLA_V] = (acc[:MLA_V] / acc[MLA_V:MLA_V + 1]).T
        return 0

    state = tuple((jnp.full((1, t), NEG_BIG, F32), jnp.zeros((1, t), F32), jnp.zeros((1, t), F32))
                  for _ in heads)
    single = qi % 2
    state = lax.cond(single == 1, lambda st: key_tile(0, diagonal(st)), diagonal, state)
    n_groups = (qi - single) // MLA_UNROLL
    state = lax.fori_loop(
        0, n_groups, lambda kk, st: key_tiles(single + MLA_UNROLL * kk, MLA_UNROLL, st), state)
    pair_start = single + MLA_UNROLL * n_groups
    lax.cond(qi - pair_start == 2, lambda st: finish(key_tiles(pair_start, 2, st)), finish, state)


def _mla_call(qt, ka, vt):
    b, nq, _, t = qt.shape
    s = ka.shape[1]
    qw = MLA_STREAMS * MLA_QK_PAD
    vw = MLA_STREAMS * MLA_V_ROWS
    ow = MLA_STREAMS * MLA_V
    return pl.pallas_call(
        _mla_kernel,
        grid=(b, MLA_HEADS // MLA_STREAMS, nq),
        in_specs=[pl.BlockSpec((1, 1, qw, t), lambda bi, h, i: (bi, i, h, 0)),
                  pl.BlockSpec((1, t, qw), lambda bi, h, i: (bi, i, h)),
                  pl.BlockSpec((1, 1, vw, t), lambda bi, h, i: (bi, i, h, 0))],
        out_specs=pl.BlockSpec((1, t, ow), lambda bi, h, i: (bi, i, h)),
        out_shape=jax.ShapeDtypeStruct((b, s, MLA_WIDTH), F32),
        scratch_shapes=[pltpu.VMEM((s, qw), BF16),
                        pltpu.VMEM((nq, vw, t), BF16),
                        pltpu.VMEM((MLA_STREAMS, t // 2, t + LANES), F32),
                        pltpu.VMEM((MLA_STREAMS, t // 2, t + LANES), F32),
                        pltpu.VMEM((MLA_STREAMS, t // 2, t + LANES), BF16),
                        pltpu.VMEM((MLA_STREAMS, t // 2, t + LANES), BF16),
                        pltpu.VMEM((MLA_STREAMS, MLA_V_ROWS, t + LANES), F32)],
        compiler_params=pltpu.CompilerParams(
            dimension_semantics=("arbitrary", "arbitrary", "arbitrary"),
            vmem_limit_bytes=VMEM_LIMIT),
        name="mla_attn",
    )(qt, ka, vt)


def _swa_kernel(sink_ref, q_ref, kc_ref, kp_ref, vc_ref, vp_ref, o_ref, kf_ref):
    blk = SWA_BLOCK
    t = q_ref.shape[3]
    n_blk = t // blk
    tile = pl.program_id(1)
    kf_ref[0:blk, :] = kp_ref[0]
    kf_ref[blk:, :] = kc_ref[0]

    cols = SWA_GROUP * blk
    k_rel = lax.broadcasted_iota(jnp.int32, (2 * blk, cols), 0)
    q_rel = lax.broadcasted_iota(jnp.int32, (2 * blk, cols), 1) & (blk - 1)
    in_window = (k_rel > q_rel) & (k_rel <= q_rel + blk)
    head_of_col = lax.broadcasted_iota(jnp.int32, (1, cols), 1) // blk

    sinks = []
    for c in range(SWA_KV_HEADS):
        sink = jnp.zeros((1, cols), F32)
        for g in range(SWA_GROUP):
            sink = jnp.where(head_of_col == g, sink_ref[c * SWA_GROUP + g] * LOG2E, sink)
        sinks.append(sink)

    def band_scores(c, n):
        q_t = jnp.concatenate(
            [q_ref[0, 0, (c * SWA_GROUP + g) * LANES:(c * SWA_GROUP + g + 1) * LANES, n * blk:(n + 1) * blk]
             for g in range(SWA_GROUP)], axis=1)
        k = kf_ref[n * blk:(n + 2) * blk, c * LANES:(c + 1) * LANES]
        return _dot(k, q_t)

    def band_softmax(c, n, s):
        valid = in_window
        if n == 0:
            valid = valid & ((k_rel >= blk) | (tile > 0))
        s = jnp.where(valid, s, NEG_BIG)
        m = jnp.maximum(jnp.max(s, axis=0, keepdims=True), sinks[c])
        return jnp.exp2(s - m).astype(BF16), m

    def band_output(c, n, p, m):
        v_rows = slice(c * SWA_V_ROWS, (c + 1) * SWA_V_ROWS)
        if n == 0:
            v_t = jnp.concatenate([vp_ref[0, 0, v_rows, :], vc_ref[0, 0, v_rows, 0:blk]], axis=1)
        else:
            v_t = vc_ref[0, 0, v_rows, (n - 1) * blk:(n + 1) * blk]
        o_t = _dot(v_t, p)
        denom = o_t[SWA_HEAD_DIM:SWA_HEAD_DIM + 1] + jnp.exp2(sinks[c] - m)
        o = o_t[:SWA_HEAD_DIM] / denom
        for j in range(SWA_GROUP // 2):
            pair_t = jnp.concatenate([o[:, (2 * j) * blk:(2 * j + 1) * blk],
                                      o[:, (2 * j + 1) * blk:(2 * j + 2) * blk]], axis=0)
            col = (c * SWA_GROUP // 2 + j) * LANES
            o_ref[0, n * blk:(n + 1) * blk, col:col + LANES] = pair_t.T

    bands = [(c, n) for c in range(SWA_KV_HEADS) for n in range(n_blk)]
    score_lead, softmax_lead = 3, 1
    s_vals, p_vals = {}, {}
    for step in range(len(bands) + score_lead):
        done = step - score_lead
        if done >= 0:
            band_output(*bands[done], *p_vals.pop(done))
        if step < len(bands):
            s_vals[step] = band_scores(*bands[step])
        ready = step - (score_lead - softmax_lead)
        if 0 <= ready < len(bands):
            p_vals[ready] = band_softmax(*bands[ready], s_vals.pop(ready))


def _swa_call(qbt, kb, vbt, sinks):
    b, n_tiles, qh, t = qbt.shape
    s = kb.shape[1]
    ratio = t // SWA_BLOCK
    kw = SWA_KV_HEADS * LANES
    vh = SWA_KV_HEADS * SWA_V_ROWS
    prev_tile = lambda i: jnp.maximum(i - 1, 0)
    return pl.pallas_call(
        _swa_kernel,
        grid=(b, n_tiles),
        in_specs=[pl.BlockSpec(memory_space=pltpu.SMEM),
                  pl.BlockSpec((1, 1, qh, t), lambda bi, i: (bi, i, 0, 0)),
                  pl.BlockSpec((1, t, kw), lambda bi, i: (bi, i, 0)),
                  pl.BlockSpec((1, SWA_BLOCK, kw), lambda bi, i: (bi, jnp.maximum(i * ratio - 1, 0), 0)),
                  pl.BlockSpec((1, 1, vh, t), lambda bi, i: (bi, i, 0, 0)),
                  pl.BlockSpec((1, 1, vh, SWA_BLOCK), lambda bi, i: (bi, prev_tile(i), 0, ratio - 1))],
        out_specs=pl.BlockSpec((1, t, SWA_WIDTH), lambda bi, i: (bi, i, 0)),
        out_shape=jax.ShapeDtypeStruct((b, s, SWA_WIDTH), F32),
        scratch_shapes=[pltpu.VMEM((t + SWA_BLOCK, kw), BF16)],
        compiler_params=pltpu.CompilerParams(
            dimension_semantics=("arbitrary", "arbitrary"), vmem_limit_bytes=VMEM_LIMIT),
        name="swa_attn",
    )(sinks, qbt, kb, kb, vbt, vbt)


def _out_ffn_kernel(x_ref, oa_ref, ob_ref, ga_ref, gb_ref, wo_ref, g_ref, wg_ref, wu_ref, wd_ref, o_ref):
    na = _rms(oa_ref[...], ga_ref[0], MLA_WIDTH).astype(BF16)
    nb = _rms(ob_ref[...], gb_ref[0], SWA_WIDTH).astype(BF16)
    y = _dot(na, wo_ref[0, 0:MLA_WIDTH, :]) + _dot(nb, wo_ref[0, MLA_WIDTH:, :])
    x_new = []
    _interleave(_ffn_stages(x_ref[...] + y, g_ref[0], wg_ref, wu_ref, wd_ref, x_new))
    o_ref[...] = x_new[0]


def _out_ffn_call(x, oa, ob, layer, ga, gb, wo, ffn_consts):
    n = x.shape[0]
    tm = min(TM_ROWS, n)
    row_spec = lambda width: pl.BlockSpec((tm, width), lambda i: (i, 0))
    return pl.pallas_call(
        _out_ffn_kernel,
        grid=(n // tm,),
        in_specs=[row_spec(D_MODEL), row_spec(MLA_WIDTH), row_spec(SWA_WIDTH),
                  _layer_spec(ga.shape, layer), _layer_spec(gb.shape, layer), _layer_spec(wo.shape, layer)]
                 + [_layer_spec(c.shape, layer) for c in ffn_consts],
        out_specs=row_spec(D_MODEL),
        out_shape=jax.ShapeDtypeStruct(x.shape, F32),
        compiler_params=pltpu.CompilerParams(
            dimension_semantics=("arbitrary",), vmem_limit_bytes=VMEM_LIMIT),
        name="out_ffn",
    )(x, oa, ob, ga, gb, wo, *ffn_consts)


def _take_cols(w, cols):
    cols = np.asarray(cols)
    cuts = [0] + [i for i in range(1, len(cols)) if cols[i] != cols[i - 1] + 1] + [len(cols)]
    runs = [w[..., int(cols[a]):int(cols[b - 1]) + 1] for a, b in zip(cuts[:-1], cuts[1:])]
    return runs[0] if len(runs) == 1 else jnp.concatenate(runs, axis=-1)


def _dup_cols(n_heads, head_dim, base=0):
    half = head_dim // 2
    cols = []
    for hd in range(n_heads):
        for part in range(2):
            start = base + hd * head_dim + part * half
            cols.extend(list(range(start, start + half)) * 2)
    return np.asarray(cols, np.int32)


def _proj_consts(mix_norm, w_in, mla_q_a_norm, mla_w_q_b, mla_kv_a_norm, mla_w_kv_b,
                 mla_q_norm, mla_k_norm, swa_q_norm, swa_k_norm):
    o_kpe = MLA_Q_RANK + MLA_KV_RANK
    o_qs = o_kpe + MLA_ROPE
    o_ks = o_qs + SWA_WIDTH
    o_vs = o_ks + SWA_KV_HEADS * SWA_HEAD_DIM
    in_cols = np.concatenate([
        np.arange(0, o_kpe, dtype=np.int32),
        _dup_cols(1, MLA_ROPE, o_kpe),
        _dup_cols(SWA_KV_HEADS, SWA_HEAD_DIM, o_ks)])
    ft_cols = np.concatenate([
        np.arange(o_qs, o_ks, dtype=np.int32),
        np.arange(o_vs, o_vs + SWA_KV_HEADS * SWA_HEAD_DIM, dtype=np.int32)])
    k_cols = np.concatenate(
        [np.arange(h * (MLA_NOPE + MLA_V), h * (MLA_NOPE + MLA_V) + MLA_NOPE) for h in range(MLA_HEADS)])
    v_cols = np.concatenate(
        [np.arange(h * (MLA_NOPE + MLA_V) + MLA_NOPE, (h + 1) * (MLA_NOPE + MLA_V)) for h in range(MLA_HEADS)])
    rope_gain = _dup_cols(1, MLA_ROPE, MLA_NOPE)
    swa_gain = _dup_cols(1, SWA_HEAD_DIM)
    t = lambda w: jnp.swapaxes(w, 1, 2)
    lane_rep = lambda v: jnp.broadcast_to(v.astype(F32)[:, :, None], v.shape + (LANES,))
    return (
        _rows(mix_norm),
        _take_cols(w_in, in_cols).astype(BF16),
        t(_take_cols(w_in, ft_cols)).astype(BF16),
        _rows(mla_q_a_norm),
        t(mla_w_q_b).astype(BF16),
        _rows(mla_kv_a_norm),
        _take_cols(mla_w_kv_b, k_cols).astype(BF16),
        t(_take_cols(mla_w_kv_b, v_cols)).astype(BF16),
        lane_rep(mla_q_norm),
        _rows(mla_k_norm[:, :MLA_NOPE]),
        _rows(_take_cols(mla_k_norm, rope_gain)),
        lane_rep(swa_q_norm),
        _rows(_take_cols(swa_k_norm, swa_gain)),
    )


def _rows(v):
    return v[:, None, :].astype(F32)


def _rope_tables(seq):
    pos = jnp.arange(seq, dtype=F32)
    inv = 1.0 / (ROPE_THETA ** (jnp.arange(0, 2 * HALF, 2, dtype=F32) / (2 * HALF)))
    ang = pos[:, None] * inv[None, :]
    c, s = jnp.cos(ang), jnp.sin(ang)
    return (jnp.concatenate([c, c, c, c], axis=1), jnp.concatenate([-s, -s, s, s], axis=1), c.T, s.T)


def kernel(x, ffn1_norm, ffn1_w_gate, ffn1_w_up, ffn1_w_down, mix_norm, w_in, mla_q_a_norm, mla_w_q_b, mla_kv_a_norm, mla_w_kv_b, mla_q_norm, mla_k_norm, swa_q_norm, swa_k_norm, swa_sinks, mla_out_norm, swa_out_norm, w_o, ffn2_norm, ffn2_w_gate, ffn2_w_up, ffn2_w_down):
    b, s, d = x.shape
    depth = w_in.shape[0]
    tables = _rope_tables(s)
    ffn1 = (_rows(ffn1_norm), ffn1_w_gate.astype(BF16), ffn1_w_up.astype(BF16), ffn1_w_down.astype(BF16))
    ffn2 = (_rows(ffn2_norm), ffn2_w_gate.astype(BF16), ffn2_w_up.astype(BF16), ffn2_w_down.astype(BF16))
    consts = _proj_consts(mix_norm, w_in, mla_q_a_norm, mla_w_q_b, mla_kv_a_norm, mla_w_kv_b,
                          mla_q_norm, mla_k_norm, swa_q_norm, swa_k_norm)
    out_consts = (_rows(mla_out_norm), _rows(swa_out_norm), w_o.astype(BF16))
    xf = x.reshape(b * s, d)
    for l in range(depth):
        xf, qt, ka, vt, qbt, kb, vbt = _ffn_proj_call(xf, s, l, ffn1, consts, tables)
        shp = lambda t: t.reshape(b, s, t.shape[-1])
        tiles = lambda t: t.reshape(b, -1, t.shape[-2], t.shape[-1])
        oa = _mla_call(tiles(qt), shp(ka), tiles(vt))
        ob = _swa_call(tiles(qbt), shp(kb), tiles(vbt), swa_sinks[l].astype(F32))
        xf = _out_ffn_call(xf, oa.reshape(b * s, MLA_WIDTH), ob.reshape(b * s, SWA_WIDTH), l,
                           *out_consts, ffn2)
    return xf.reshape(b, s, d)
```

```python
import functools

import numpy as np
import jax
import jax.numpy as jnp
from jax import lax
from jax.experimental import pallas as pl
from jax.experimental.pallas import tpu as pltpu

D_MODEL = 1024
EPS = 1e-6
ROPE_THETA = 10000.0
MLA_HEADS = 4
MLA_Q_RANK = 256
MLA_KV_RANK = 128
MLA_NOPE = 128
MLA_ROPE = 64
MLA_V = 128
MLA_QK = MLA_NOPE + MLA_ROPE
MLA_WIDTH = MLA_HEADS * MLA_V
SWA_HEADS = 8
SWA_KV_HEADS = 2
SWA_GROUP = SWA_HEADS // SWA_KV_HEADS
SWA_HEAD_DIM = 64
SWA_BLOCK = 128
SWA_WIDTH = SWA_HEADS * SWA_HEAD_DIM
D_FF = 2816

LANES = 128
HALF = SWA_HEAD_DIM // 2
MLA_QK_PAD = 2 * LANES
BF16_ROWS = 16
MLA_V_ROWS = MLA_V + BF16_ROWS
NEG_BIG = -1e30

VMEM_LIMIT = 56 * 1024 * 1024

LOG2E = 1.4426950408889634

TM_ROWS = 512
SWA_TILES_PER_STEP = 4

BF16 = jnp.bfloat16
F32 = jnp.float32


def _rms(t, gain, width):
    ss = jnp.sum(t * t, axis=-1, keepdims=True)
    return t * lax.rsqrt(ss * (1.0 / width) + EPS) * gain


def _dot(a, b):
    return jnp.dot(a, b, preferred_element_type=F32)


def _dot_nt(a, b):
    return lax.dot_general(a, b, (((1,), (1,)), ((), ())), preferred_element_type=F32)


def _ffn_stages(x, g, wg_ref, wu_ref, wd_ref, out):
    h = _rms(x, g, D_MODEL).astype(BF16)
    gate = _dot(h, wg_ref[0])
    up = _dot(h, wu_ref[0])
    yield
    half_gate = 0.5 * gate
    gated = half_gate * up
    act = (gated + gated * jnp.tanh(half_gate)).astype(BF16)
    out.append(x + 0.5 * _dot(act, wd_ref[0]))


def _interleave(*staged):
    active = list(staged)
    while active:
        for gen in list(active):
            try:
                next(gen)
            except StopIteration:
                active.remove(gen)


def _layer_spec(stacked_shape, layer):
    nd = len(stacked_shape)
    return pl.BlockSpec((1,) + tuple(stacked_shape[1:]), lambda *_: (layer,) + (0,) * (nd - 1),
                        pipeline_mode=pl.Buffered(1))


C_CQ = 0
C_CKV = C_CQ + MLA_Q_RANK
C_KPE = C_CKV + MLA_KV_RANK
C_KS = C_KPE + LANES
C_END = C_KS + SWA_KV_HEADS * LANES
R_VS = SWA_HEADS * SWA_HEAD_DIM
R_END = R_VS + SWA_KV_HEADS * SWA_HEAD_DIM
SWA_V_ROWS = SWA_HEAD_DIM + BF16_ROWS


def _rope(t, cos, sin_signed):
    return t * cos + pltpu.roll(t, 2 * HALF, 1) * sin_signed


def _proj_stages(x, g_ref, win_ref, wft_ref, gqa_ref, wqbt_ref, gkva_ref, wkb_ref, wvt_ref,
                 gq_ref, gkn_ref, gkr_ref, gsq_ref, gsk_ref, cos_ref, sin_ref, cost_ref, sint_ref,
                 qt_ref, ka_ref, vt_ref, qbt_ref, kb_ref, vbt_ref):
    tm = x.shape[0]
    q_scale = MLA_QK ** -0.5 * LOG2E
    swa_scale = SWA_HEAD_DIM ** -0.5 * LOG2E

    h = _rms(x, g_ref[0], D_MODEL).astype(BF16)
    proj = _dot(h, win_ref[0])
    proj_t = _dot_nt(wft_ref[0], h)
    yield

    cq_t = _rms(proj[:, C_CQ:C_CKV], gqa_ref[0], MLA_Q_RANK).T.astype(BF16)
    ckv = _rms(proj[:, C_CKV:C_KPE], gkva_ref[0], MLA_KV_RANK)
    q_up = _dot(wqbt_ref[0], cq_t)
    k_up = _dot(ckv.astype(BF16), wkb_ref[0])
    v_t = _dot(wvt_ref[0], ckv.T.astype(BF16)).astype(BF16)
    yield

    cos = cos_ref[...]
    sin = sin_ref[...]
    cos_t = cost_ref[...]
    sin_t = sint_ref[...]
    zeros = jnp.zeros((HALF, tm), BF16)
    ones_row = (lax.broadcasted_iota(jnp.int32, (BF16_ROWS, tm), 0) == 0).astype(BF16)

    gq = jnp.tile(gq_ref[0], (1, tm // LANES))
    for hd in range(MLA_HEADS):
        blk = q_up[hd * MLA_QK:(hd + 1) * MLA_QK]
        ss = jnp.sum(blk * blk, axis=0, keepdims=True)
        qn = blk * (lax.rsqrt(ss * (1.0 / MLA_QK) + EPS) * q_scale) * gq
        t1 = qn[MLA_NOPE:MLA_NOPE + HALF]
        t2 = qn[MLA_NOPE + HALF:]
        base = hd * MLA_QK_PAD
        qt_ref[0, base:base + MLA_NOPE, :] = qn[:MLA_NOPE].astype(BF16)
        qt_ref[0, base + MLA_NOPE:base + MLA_NOPE + HALF, :] = (t1 * cos_t - t2 * sin_t).astype(BF16)
        qt_ref[0, base + MLA_NOPE + HALF:base + MLA_NOPE + 2 * HALF, :] = zeros
        qt_ref[0, base + MLA_NOPE + 2 * HALF:base + MLA_NOPE + 3 * HALF, :] = (
            t2 * cos_t + t1 * sin_t).astype(BF16)
        qt_ref[0, base + MLA_NOPE + 3 * HALF:base + MLA_QK_PAD, :] = zeros

    for hd in range(MLA_HEADS):
        vt_ref[0, hd * MLA_V_ROWS:hd * MLA_V_ROWS + MLA_V, :] = v_t[hd * MLA_V:(hd + 1) * MLA_V]
        vt_ref[0, hd * MLA_V_ROWS + MLA_V:(hd + 1) * MLA_V_ROWS, :] = ones_row
    kpe = proj[:, C_KPE:C_KS]
    ss_pe = 0.5 * jnp.sum(kpe * kpe, axis=-1, keepdims=True)
    kpe_rot = _rope(kpe * gkr_ref[0], cos, sin)
    for hd in range(MLA_HEADS):
        nope = k_up[:, hd * MLA_NOPE:(hd + 1) * MLA_NOPE]
        ss = jnp.sum(nope * nope, axis=-1, keepdims=True) + ss_pe
        r = lax.rsqrt(ss * (1.0 / MLA_QK) + EPS)
        ka_ref[:, hd * MLA_QK_PAD: hd * MLA_QK_PAD + MLA_NOPE] = (
            nope * r * gkn_ref[0]).astype(BF16)
        ka_ref[:, hd * MLA_QK_PAD + MLA_NOPE:(hd + 1) * MLA_QK_PAD] = (kpe_rot * r).astype(BF16)

    gsq = jnp.tile(gsq_ref[0], (1, tm // LANES))
    for hd in range(SWA_HEADS):
        blk = proj_t[hd * SWA_HEAD_DIM:(hd + 1) * SWA_HEAD_DIM]
        ss = jnp.sum(blk * blk, axis=0, keepdims=True)
        qn = blk * (lax.rsqrt(ss * (1.0 / SWA_HEAD_DIM) + EPS) * swa_scale) * gsq
        t1 = qn[:HALF]
        t2 = qn[HALF:]
        base = hd * LANES
        qbt_ref[0, base:base + HALF, :] = (t1 * cos_t - t2 * sin_t).astype(BF16)
        qbt_ref[0, base + HALF:base + 2 * HALF, :] = zeros
        qbt_ref[0, base + 2 * HALF:base + 3 * HALF, :] = (t2 * cos_t + t1 * sin_t).astype(BF16)
        qbt_ref[0, base + 3 * HALF:base + LANES, :] = zeros
    for c in range(SWA_KV_HEADS):
        vbt_ref[0, c * SWA_V_ROWS:c * SWA_V_ROWS + SWA_HEAD_DIM, :] = (
            proj_t[R_VS + c * SWA_HEAD_DIM:R_VS + (c + 1) * SWA_HEAD_DIM].astype(BF16))
        vbt_ref[0, c * SWA_V_ROWS + SWA_HEAD_DIM:(c + 1) * SWA_V_ROWS, :] = ones_row
    for c in range(SWA_KV_HEADS):
        xk = proj[:, C_KS + c * LANES: C_KS + (c + 1) * LANES]
        ss = 0.5 * jnp.sum(xk * xk, axis=-1, keepdims=True)
        r = lax.rsqrt(ss * (1.0 / SWA_HEAD_DIM) + EPS)
        kb_ref[:, c * LANES:(c + 1) * LANES] = _rope(xk * r * gsk_ref[0], cos, sin).astype(BF16)


N_FFN_CONSTS = 4
N_PROJ_CONSTS = 13
N_PROJ_TABLES = 4


def _ffn_proj_kernel(*refs):
    x_ref = refs[0]
    ffn_refs = refs[1:1 + N_FFN_CONSTS]
    proj_refs = refs[1 + N_FFN_CONSTS:1 + N_FFN_CONSTS + N_PROJ_CONSTS + N_PROJ_TABLES]
    o_ref = refs[1 + N_FFN_CONSTS + N_PROJ_CONSTS + N_PROJ_TABLES]
    proj_out_refs = refs[-7:-1]
    x_prev = refs[-1]

    @pl.when(pl.program_id(0) == 0)
    def _():
        x_prev[...] = jnp.zeros_like(x_prev)

    g_ref, wg_ref, wu_ref, wd_ref = ffn_refs
    x_new = []
    _interleave(_proj_stages(x_prev[...], *proj_refs, *proj_out_refs),
                _ffn_stages(x_ref[...], g_ref[0], wg_ref, wu_ref, wd_ref, x_new))
    o_ref[...] = x_new[0]
    x_prev[...] = x_new[0]


def _ffn_proj_call(x, seq, layer, ffn_consts, consts, tables):
    n = x.shape[0]
    tm = min(TM_ROWS, seq)
    n_tiles = n // tm
    n_seq_tiles = seq // tm
    ffn_tile = lambda i: jnp.minimum(i, n_tiles - 1)
    proj_tile = lambda i: jnp.maximum(i - 1, 0)
    proj_row_spec = lambda width: pl.BlockSpec((tm, width), lambda i: (proj_tile(i), 0))
    ffn_row_spec = pl.BlockSpec((tm, D_MODEL), lambda i: (ffn_tile(i), 0))
    pos_spec = pl.BlockSpec((tm, LANES), lambda i: (proj_tile(i) % n_seq_tiles, 0))
    pos_t_spec = pl.BlockSpec((HALF, tm), lambda i: (0, proj_tile(i) % n_seq_tiles))
    row_widths = {1: MLA_HEADS * MLA_QK_PAD, 4: SWA_KV_HEADS * LANES}
    col_heights = {0: MLA_HEADS * MLA_QK_PAD, 2: MLA_HEADS * MLA_V_ROWS,
                   3: SWA_HEADS * LANES, 5: SWA_KV_HEADS * SWA_V_ROWS}
    out_specs, out_shape = [ffn_row_spec], [jax.ShapeDtypeStruct(x.shape, F32)]
    for idx in range(6):
        if idx in row_widths:
            out_specs.append(proj_row_spec(row_widths[idx]))
            out_shape.append(jax.ShapeDtypeStruct((n, row_widths[idx]), BF16))
        else:
            out_specs.append(pl.BlockSpec((1, col_heights[idx], tm), lambda i: (proj_tile(i), 0, 0)))
            out_shape.append(jax.ShapeDtypeStruct((n_tiles, col_heights[idx], tm), BF16))
    assert len(ffn_consts) == N_FFN_CONSTS and len(consts) == N_PROJ_CONSTS and len(tables) == N_PROJ_TABLES
    return pl.pallas_call(
        _ffn_proj_kernel,
        grid=(n_tiles + 1,),
        in_specs=([ffn_row_spec] + [_layer_spec(c.shape, layer) for c in ffn_consts]
                  + [_layer_spec(c.shape, layer) for c in consts]
                  + [pos_spec, pos_spec, pos_t_spec, pos_t_spec]),
        out_specs=out_specs,
        out_shape=out_shape,
        scratch_shapes=[pltpu.VMEM((tm, D_MODEL), F32)],
        compiler_params=pltpu.CompilerParams(
            dimension_semantics=("arbitrary",), vmem_limit_bytes=VMEM_LIMIT),
        name="ffn_proj",
    )(x, *ffn_consts, *consts, *tables)


MLA_UNROLL = 4
MLA_STREAMS = 4


def _mla_kernel(q_ref, k_tile_ref, v_tile_ref, o_ref, k_ref, v_ref, s0_ref, s1_ref, p0_ref, p1_ref, acc_ref):
    t = q_ref.shape[3]
    tk = t // 2
    qi = pl.program_id(2)
    heads = range(MLA_STREAMS)
    s_refs = (s0_ref, s1_ref)
    p_refs = (p0_ref, p1_ref)

    k_ref[pl.ds(pl.multiple_of(qi * t, t), t), :] = k_tile_ref[0]
    v_ref[qi] = v_tile_ref[0, 0]

    def scores(tile, half, h):
        start = pl.multiple_of(tile * t, t) + half * tk
        k = k_ref[pl.ds(start, tk), h * MLA_QK_PAD:(h + 1) * MLA_QK_PAD]
        return _dot(k, q_ref[0, 0, h * MLA_QK_PAD:(h + 1) * MLA_QK_PAD, :])

    def weighted_values(tile, half, h, buf):
        v = v_ref[tile, h * MLA_V_ROWS:(h + 1) * MLA_V_ROWS, half * tk:(half + 1) * tk]
        return _dot(v, p_refs[buf][h, :, :t])

    def phase(cur, state, nxt_tile, prev_tile, mask_half=None):
        new_state = []
        for h in heads:
            m, alpha_prev, block_max = state[h]
            if prev_tile is not None:
                pv = weighted_values(prev_tile, 1 - cur, h, 1 - cur)
            next_max = block_max
            if nxt_tile is not None:
                nxt = scores(nxt_tile, 1 - cur, h)
                s_refs[1 - cur][h, :, :t] = nxt
                next_max = jnp.max(nxt, axis=0, keepdims=True)
            s = s_refs[cur][h, :, :t]
            if mask_half is not None:
                key = lax.broadcasted_iota(jnp.int32, (tk, t), 0) + mask_half * tk
                qry = lax.broadcasted_iota(jnp.int32, (tk, t), 1)
                s = jnp.where(key <= qry, s, NEG_BIG)
                block_max = jnp.max(s, axis=0, keepdims=True)
            m_new = jnp.maximum(m, block_max)
            p_refs[cur][h, :, :t] = jnp.exp2(s - m_new).astype(BF16)
            alpha = jnp.exp2(m - m_new)
            if prev_tile is not None:
                acc_ref[h, :, :t] = alpha_prev * acc_ref[h, :, :t] + pv
            new_state.append((m_new, alpha, next_max))
        return tuple(new_state)

    def diagonal(state):
        for h in heads:
            acc_ref[h, :, :t] = jnp.zeros((MLA_V_ROWS, t), F32)
            s_refs[0][h, :, :t] = scores(qi, 0, h)
        state = phase(0, state, qi, None, mask_half=0)
        return phase(1, state, 0, qi, mask_half=1)

    def key_tile(jj, state):
        state = phase(0, state, jj, jnp.where(jj == 0, qi, jj - 1))
        return phase(1, state, jj + 1, jj)

    def key_tiles(first, count, state):
        for j in range(count):
            state = key_tile(first + j, state)
        return state

    def finish(state):
        last_tile = jnp.maximum(qi - 1, 0)
        for h in heads:
            alpha_prev = state[h][1]
            acc = alpha_prev * acc_ref[h, :, :t] + weighted_values(last_tile, 1, h, 1)
            o_ref[0, :, h * MLA_V:(h + 1) * MLA_V] = (acc[:MLA_V] / acc[MLA_V:MLA_V + 1]).T
        return 0

    state = tuple((jnp.full((1, t), NEG_BIG, F32), jnp.zeros((1, t), F32), jnp.zeros((1, t), F32))
                  for _ in heads)
    single = qi % 2
    state = lax.cond(single == 1, lambda st: key_tile(0, diagonal(st)), diagonal, state)
    n_groups = (qi - single) // MLA_UNROLL
    state = lax.fori_loop(
        0, n_groups, lambda kk, st: key_tiles(single + MLA_UNROLL * kk, MLA_UNROLL, st), state)
    pair_start = single + MLA_UNROLL * n_groups
    lax.cond(qi - pair_start == 2, lambda st: finish(key_tiles(pair_start, 2, st)), finish, state)


def _mla_call(qt, ka, vt):
    b, nq, _, t = qt.shape
    s = ka.shape[1]
    qw = MLA_STREAMS * MLA_QK_PAD
    vw = MLA_STREAMS * MLA_V_ROWS
    ow = MLA_STREAMS * MLA_V
    return pl.pallas_call(
        _mla_kernel,
        grid=(b, MLA_HEADS // MLA_STREAMS, nq),
        in_specs=[pl.BlockSpec((1, 1, qw, t), lambda bi, h, i: (bi, i, h, 0)),
                  pl.BlockSpec((1, t, qw), lambda bi, h, i: (bi, i, h)),
                  pl.BlockSpec((1, 1, vw, t), lambda bi, h, i: (bi, i, h, 0))],
        out_specs=pl.BlockSpec((1, t, ow), lambda bi, h, i: (bi, i, h)),
        out_shape=jax.ShapeDtypeStruct((b, s, MLA_WIDTH), F32),
        scratch_shapes=[pltpu.VMEM((s, qw), BF16),
                        pltpu.VMEM((nq, vw, t), BF16),
                        pltpu.VMEM((MLA_STREAMS, t // 2, t + LANES), F32),
                        pltpu.VMEM((MLA_STREAMS, t // 2, t + LANES), F32),
                        pltpu.VMEM((MLA_STREAMS, t // 2, t + LANES), BF16),
                        pltpu.VMEM((MLA_STREAMS, t // 2, t + LANES), BF16),
                        pltpu.VMEM((MLA_STREAMS, MLA_V_ROWS, t + LANES), F32)],
        compiler_params=pltpu.CompilerParams(
            dimension_semantics=("arbitrary", "arbitrary", "arbitrary"),
            vmem_limit_bytes=VMEM_LIMIT),
        name="mla_attn",
    )(qt, ka, vt)


def _swa_kernel(sink_ref, q_ref, kc_ref, kp_ref, vc_ref, vp_ref, o_ref, kf_ref):
    blk = SWA_BLOCK
    tiles, t = q_ref.shape[1], q_ref.shape[3]
    per_tile = t // blk
    n_blk = tiles * per_tile
    tile = pl.program_id(1)
    kf_ref[0:blk, :] = kp_ref[0]
    kf_ref[blk:, :] = kc_ref[0]

    cols = SWA_GROUP * blk
    k_rel = lax.broadcasted_iota(jnp.int32, (2 * blk, cols), 0)
    q_rel = lax.broadcasted_iota(jnp.int32, (2 * blk, cols), 1) & (blk - 1)
    in_window = (k_rel > q_rel) & (k_rel <= q_rel + blk)
    head_of_col = lax.broadcasted_iota(jnp.int32, (1, cols), 1) // blk

    sinks = []
    for c in range(SWA_KV_HEADS):
        sink = jnp.zeros((1, cols), F32)
        for g in range(SWA_GROUP):
            sink = jnp.where(head_of_col == g, sink_ref[c * SWA_GROUP + g] * LOG2E, sink)
        sinks.append(sink)

    def band_scores(c, n):
        q_t = jnp.concatenate(
            [q_ref[0, n // per_tile, (c * SWA_GROUP + g) * LANES:(c * SWA_GROUP + g + 1) * LANES,
                   (n % per_tile) * blk:(n % per_tile + 1) * blk]
             for g in range(SWA_GROUP)], axis=1)
        k = kf_ref[n * blk:(n + 2) * blk, c * LANES:(c + 1) * LANES]
        return _dot(k, q_t)

    def band_softmax(c, n, s):
        valid = in_window
        if n == 0:
            valid = valid & ((k_rel >= blk) | (tile > 0))
        s = jnp.where(valid, s, NEG_BIG)
        m = jnp.maximum(jnp.max(s, axis=0, keepdims=True), sinks[c])
        return jnp.exp2(s - m).astype(BF16), m

    def band_output(c, n, p, m):
        v_rows = slice(c * SWA_V_ROWS, (c + 1) * SWA_V_ROWS)
        a, local = n // per_tile, n % per_tile
        if n == 0:
            v_t = jnp.concatenate([vp_ref[0, 0, v_rows, :], vc_ref[0, 0, v_rows, 0:blk]], axis=1)
        elif local == 0:
            v_t = jnp.concatenate([vc_ref[0, a - 1, v_rows, t - blk:t], vc_ref[0, a, v_rows, 0:blk]], axis=1)
        else:
            v_t = vc_ref[0, a, v_rows, (local - 1) * blk:(local + 1) * blk]
        o_t = _dot(v_t, p)
        denom = o_t[SWA_HEAD_DIM:SWA_HEAD_DIM + 1] + jnp.exp2(sinks[c] - m)
        o = o_t[:SWA_HEAD_DIM] / denom
        for j in range(SWA_GROUP // 2):
            pair_t = jnp.concatenate([o[:, (2 * j) * blk:(2 * j + 1) * blk],
                                      o[:, (2 * j + 1) * blk:(2 * j + 2) * blk]], axis=0)
            col = (c * SWA_GROUP // 2 + j) * LANES
            o_ref[0, n * blk:(n + 1) * blk, col:col + LANES] = pair_t.T

    bands = [(c, n) for c in range(SWA_KV_HEADS) for n in range(n_blk)]
    score_lead, softmax_lead = 3, 1
    s_vals, p_vals = {}, {}
    for step in range(len(bands) + score_lead):
        done = step - score_lead
        if done >= 0:
            band_output(*bands[done], *p_vals.pop(done))
        if step < len(bands):
            s_vals[step] = band_scores(*bands[step])
        ready = step - (score_lead - softmax_lead)
        if 0 <= ready < len(bands):
            p_vals[ready] = band_softmax(*bands[ready], s_vals.pop(ready))


def _swa_call(qbt, kb, vbt, sinks):
    b, n_tiles, qh, t = qbt.shape
    s = kb.shape[1]
    ratio = t // SWA_BLOCK
    tiles = min(SWA_TILES_PER_STEP, n_tiles)
    rows = tiles * t
    kw = SWA_KV_HEADS * LANES
    vh = SWA_KV_HEADS * SWA_V_ROWS
    return pl.pallas_call(
        _swa_kernel,
        grid=(b, n_tiles // tiles),
        in_specs=[pl.BlockSpec(memory_space=pltpu.SMEM),
                  pl.BlockSpec((1, tiles, qh, t), lambda bi, i: (bi, i, 0, 0)),
                  pl.BlockSpec((1, rows, kw), lambda bi, i: (bi, i, 0)),
                  pl.BlockSpec((1, SWA_BLOCK, kw),
                               lambda bi, i: (bi, jnp.maximum(i * tiles * ratio - 1, 0), 0)),
                  pl.BlockSpec((1, tiles, vh, t), lambda bi, i: (bi, i, 0, 0)),
                  pl.BlockSpec((1, 1, vh, SWA_BLOCK),
                               lambda bi, i: (bi, jnp.maximum(i * tiles - 1, 0), 0, ratio - 1))],
        out_specs=pl.BlockSpec((1, rows, SWA_WIDTH), lambda bi, i: (bi, i, 0)),
        out_shape=jax.ShapeDtypeStruct((b, s, SWA_WIDTH), F32),
        scratch_shapes=[pltpu.VMEM((rows + SWA_BLOCK, kw), BF16)],
        compiler_params=pltpu.CompilerParams(
            dimension_semantics=("arbitrary", "arbitrary"), vmem_limit_bytes=VMEM_LIMIT),
        name="swa_attn",
    )(sinks, qbt, kb, kb, vbt, vbt)


def _out_ffn_kernel(x_ref, oa_ref, ob_ref, ga_ref, gb_ref, wo_ref, g_ref, wg_ref, wu_ref, wd_ref, o_ref):
    na = _rms(oa_ref[...], ga_ref[0], MLA_WIDTH).astype(BF16)
    nb = _rms(ob_ref[...], gb_ref[0], SWA_WIDTH).astype(BF16)
    y = _dot(na, wo_ref[0, 0:MLA_WIDTH, :]) + _dot(nb, wo_ref[0, MLA_WIDTH:, :])
    x_new = []
    _interleave(_ffn_stages(x_ref[...] + y, g_ref[0], wg_ref, wu_ref, wd_ref, x_new))
    o_ref[...] = x_new[0]


def _out_ffn_call(x, oa, ob, layer, ga, gb, wo, ffn_consts):
    n = x.shape[0]
    tm = min(TM_ROWS, n)
    row_spec = lambda width: pl.BlockSpec((tm, width), lambda i: (i, 0))
    return pl.pallas_call(
        _out_ffn_kernel,
        grid=(n // tm,),
        in_specs=[row_spec(D_MODEL), row_spec(MLA_WIDTH), row_spec(SWA_WIDTH),
                  _layer_spec(ga.shape, layer), _layer_spec(gb.shape, layer), _layer_spec(wo.shape, layer)]
                 + [_layer_spec(c.shape, layer) for c in ffn_consts],
        out_specs=row_spec(D_MODEL),
        out_shape=jax.ShapeDtypeStruct(x.shape, F32),
        compiler_params=pltpu.CompilerParams(
            dimension_semantics=("arbitrary",), vmem_limit_bytes=VMEM_LIMIT),
        name="out_ffn",
    )(x, oa, ob, ga, gb, wo, *ffn_consts)


def _take_cols(w, cols):
    cols = np.asarray(cols)
    cuts = [0] + [i for i in range(1, len(cols)) if cols[i] != cols[i - 1] + 1] + [len(cols)]
    runs = [w[..., int(cols[a]):int(cols[b - 1]) + 1] for a, b in zip(cuts[:-1], cuts[1:])]
    return runs[0] if len(runs) == 1 else jnp.concatenate(runs, axis=-1)


def _dup_cols(n_heads, head_dim, base=0):
    half = head_dim // 2
    cols = []
    for hd in range(n_heads):
        for part in range(2):
            start = base + hd * head_dim + part * half
            cols.extend(list(range(start, start + half)) * 2)
    return np.asarray(cols, np.int32)


def _proj_consts(mix_norm, w_in, mla_q_a_norm, mla_w_q_b, mla_kv_a_norm, mla_w_kv_b,
                 mla_q_norm, mla_k_norm, swa_q_norm, swa_k_norm):
    o_kpe = MLA_Q_RANK + MLA_KV_RANK
    o_qs = o_kpe + MLA_ROPE
    o_ks = o_qs + SWA_WIDTH
    o_vs = o_ks + SWA_KV_HEADS * SWA_HEAD_DIM
    in_cols = np.concatenate([
        np.arange(0, o_kpe, dtype=np.int32),
        _dup_cols(1, MLA_ROPE, o_kpe),
        _dup_cols(SWA_KV_HEADS, SWA_HEAD_DIM, o_ks)])
    ft_cols = np.concatenate([
        np.arange(o_qs, o_ks, dtype=np.int32),
        np.arange(o_vs, o_vs + SWA_KV_HEADS * SWA_HEAD_DIM, dtype=np.int32)])
    k_cols = np.concatenate(
        [np.arange(h * (MLA_NOPE + MLA_V), h * (MLA_NOPE + MLA_V) + MLA_NOPE) for h in range(MLA_HEADS)])
    v_cols = np.concatenate(
        [np.arange(h * (MLA_NOPE + MLA_V) + MLA_NOPE, (h + 1) * (MLA_NOPE + MLA_V)) for h in range(MLA_HEADS)])
    rope_gain = _dup_cols(1, MLA_ROPE, MLA_NOPE)
    swa_gain = _dup_cols(1, SWA_HEAD_DIM)
    t = lambda w: jnp.swapaxes(w, 1, 2)
    lane_rep = lambda v: jnp.broadcast_to(v.astype(F32)[:, :, None], v.shape + (LANES,))
    return (
        _rows(mix_norm),
        _take_cols(w_in, in_cols).astype(BF16),
        t(_take_cols(w_in, ft_cols)).astype(BF16),
        _rows(mla_q_a_norm),
        t(mla_w_q_b).astype(BF16),
        _rows(mla_kv_a_norm),
        _take_cols(mla_w_kv_b, k_cols).astype(BF16),
        t(_take_cols(mla_w_kv_b, v_cols)).astype(BF16),
        lane_rep(mla_q_norm),
        _rows(mla_k_norm[:, :MLA_NOPE]),
        _rows(_take_cols(mla_k_norm, rope_gain)),
        lane_rep(swa_q_norm),
        _rows(_take_cols(swa_k_norm, swa_gain)),
    )


def _rows(v):
    return v[:, None, :].astype(F32)


def _rope_tables(seq):
    pos = jnp.arange(seq, dtype=F32)
    inv = 1.0 / (ROPE_THETA ** (jnp.arange(0, 2 * HALF, 2, dtype=F32) / (2 * HALF)))
    ang = pos[:, None] * inv[None, :]
    c, s = jnp.cos(ang), jnp.sin(ang)
    return (jnp.concatenate([c, c, c, c], axis=1), jnp.concatenate([-s, -s, s, s], axis=1), c.T, s.T)


def kernel(x, ffn1_norm, ffn1_w_gate, ffn1_w_up, ffn1_w_down, mix_norm, w_in, mla_q_a_norm, mla_w_q_b, mla_kv_a_norm, mla_w_kv_b, mla_q_norm, mla_k_norm, swa_q_norm, swa_k_norm, swa_sinks, mla_out_norm, swa_out_norm, w_o, ffn2_norm, ffn2_w_gate, ffn2_w_up, ffn2_w_down):
    b, s, d = x.shape
    depth = w_in.shape[0]
    tables = _rope_tables(s)
    ffn1 = (_rows(ffn1_norm), ffn1_w_gate.astype(BF16), ffn1_w_up.astype(BF16), ffn1_w_down.astype(BF16))
    ffn2 = (_rows(ffn2_norm), ffn2_w_gate.astype(BF16), ffn2_w_up.astype(BF16), ffn2_w_down.astype(BF16))
    consts = _proj_consts(mix_norm, w_in, mla_q_a_norm, mla_w_q_b, mla_kv_a_norm, mla_w_kv_b,
                          mla_q_norm, mla_k_norm, swa_q_norm, swa_k_norm)
    out_consts = (_rows(mla_out_norm), _rows(swa_out_norm), w_o.astype(BF16))
    xf = x.reshape(b * s, d)
    for l in range(depth):
        xf, qt, ka, vt, qbt, kb, vbt = _ffn_proj_call(xf, s, l, ffn1, consts, tables)
        shp = lambda t: t.reshape(b, s, t.shape[-1])
        tiles = lambda t: t.reshape(b, -1, t.shape[-2], t.shape[-1])
        oa = _mla_call(tiles(qt), shp(ka), tiles(vt))
        ob = _swa_call(tiles(qbt), shp(kb), tiles(vbt), swa_sinks[l].astype(F32))
        xf = _out_ffn_call(xf, oa.reshape(b * s, MLA_WIDTH), ob.reshape(b * s, SWA_WIDTH), l,
                           *out_consts, ffn2)
    return xf.reshape(b, s, d)
```

```python
import functools

import numpy as np
import jax
import jax.numpy as jnp
from jax import lax
from jax.experimental import pallas as pl
from jax.experimental.pallas import tpu as pltpu

D_MODEL = 1024
EPS = 1e-6
ROPE_THETA = 10000.0
MLA_HEADS = 4
MLA_Q_RANK = 256
MLA_KV_RANK = 128
MLA_NOPE = 128
MLA_ROPE = 64
MLA_V = 128
MLA_QK = MLA_NOPE + MLA_ROPE
MLA_WIDTH = MLA_HEADS * MLA_V
SWA_HEADS = 8
SWA_KV_HEADS = 2
SWA_GROUP = SWA_HEADS // SWA_KV_HEADS
SWA_HEAD_DIM = 64
SWA_BLOCK = 128
SWA_WIDTH = SWA_HEADS * SWA_HEAD_DIM
D_FF = 2816

LANES = 128
HALF = SWA_HEAD_DIM // 2
MLA_QK_PAD = 2 * LANES
BF16_ROWS = 16
MLA_V_ROWS = MLA_V + BF16_ROWS
NEG_BIG = -1e30

VMEM_LIMIT = 56 * 1024 * 1024

LOG2E = 1.4426950408889634

TM_ROWS = 512
SWA_TILES_PER_STEP = 8

BF16 = jnp.bfloat16
F32 = jnp.float32


def _rms(t, gain, width):
    ss = jnp.sum(t * t, axis=-1, keepdims=True)
    return t * lax.rsqrt(ss * (1.0 / width) + EPS) * gain


def _dot(a, b):
    return jnp.dot(a, b, preferred_element_type=F32)


def _dot_nt(a, b):
    return lax.dot_general(a, b, (((1,), (1,)), ((), ())), preferred_element_type=F32)


def _ffn_stages(x, g, wg_ref, wu_ref, wd_ref, out):
    h = _rms(x, g, D_MODEL).astype(BF16)
    gate = _dot(h, wg_ref[0])
    up = _dot(h, wu_ref[0])
    yield
    half_gate = 0.5 * gate
    gated = half_gate * up
    act = (gated + gated * jnp.tanh(half_gate)).astype(BF16)
    out.append(x + 0.5 * _dot(act, wd_ref[0]))


def _interleave(*staged):
    active = list(staged)
    while active:
        for gen in list(active):
            try:
                next(gen)
            except StopIteration:
                active.remove(gen)


def _layer_spec(stacked_shape, layer):
    nd = len(stacked_shape)
    return pl.BlockSpec((1,) + tuple(stacked_shape[1:]), lambda *_: (layer,) + (0,) * (nd - 1),
                        pipeline_mode=pl.Buffered(1))


C_CQ = 0
C_CKV = C_CQ + MLA_Q_RANK
C_KPE = C_CKV + MLA_KV_RANK
C_KS = C_KPE + LANES
C_END = C_KS + SWA_KV_HEADS * LANES
R_VS = SWA_HEADS * SWA_HEAD_DIM
R_END = R_VS + SWA_KV_HEADS * SWA_HEAD_DIM
SWA_V_ROWS = SWA_HEAD_DIM + BF16_ROWS


def _rope(t, cos, sin_signed):
    return t * cos + pltpu.roll(t, 2 * HALF, 1) * sin_signed


def _proj_stages(x, g_ref, win_ref, wft_ref, gqa_ref, wqbt_ref, gkva_ref, wkb_ref, wvt_ref,
                 gq_ref, gkn_ref, gkr_ref, gsq_ref, gsk_ref, cos_ref, sin_ref, cost_ref, sint_ref,
                 qt_ref, ka_ref, vt_ref, qbt_ref, kb_ref, vbt_ref):
    tm = x.shape[0]
    q_scale = MLA_QK ** -0.5 * LOG2E
    swa_scale = SWA_HEAD_DIM ** -0.5 * LOG2E

    h = _rms(x, g_ref[0], D_MODEL).astype(BF16)
    proj = _dot(h, win_ref[0])
    proj_t = _dot_nt(wft_ref[0], h)
    yield

    cq_t = _rms(proj[:, C_CQ:C_CKV], gqa_ref[0], MLA_Q_RANK).T.astype(BF16)
    ckv = _rms(proj[:, C_CKV:C_KPE], gkva_ref[0], MLA_KV_RANK)
    q_up = _dot(wqbt_ref[0], cq_t)
    k_up = _dot(ckv.astype(BF16), wkb_ref[0])
    v_t = _dot(wvt_ref[0], ckv.T.astype(BF16)).astype(BF16)
    yield

    cos = cos_ref[...]
    sin = sin_ref[...]
    cos_t = cost_ref[...]
    sin_t = sint_ref[...]
    zeros = jnp.zeros((HALF, tm), BF16)
    ones_row = (lax.broadcasted_iota(jnp.int32, (BF16_ROWS, tm), 0) == 0).astype(BF16)

    gq = jnp.tile(gq_ref[0], (1, tm // LANES))
    for hd in range(MLA_HEADS):
        blk = q_up[hd * MLA_QK:(hd + 1) * MLA_QK]
        ss = jnp.sum(blk * blk, axis=0, keepdims=True)
        qn = blk * (lax.rsqrt(ss * (1.0 / MLA_QK) + EPS) * q_scale) * gq
        t1 = qn[MLA_NOPE:MLA_NOPE + HALF]
        t2 = qn[MLA_NOPE + HALF:]
        base = hd * MLA_QK_PAD
        qt_ref[0, base:base + MLA_NOPE, :] = qn[:MLA_NOPE].astype(BF16)
        qt_ref[0, base + MLA_NOPE:base + MLA_NOPE + HALF, :] = (t1 * cos_t - t2 * sin_t).astype(BF16)
        qt_ref[0, base + MLA_NOPE + HALF:base + MLA_NOPE + 2 * HALF, :] = zeros
        qt_ref[0, base + MLA_NOPE + 2 * HALF:base + MLA_NOPE + 3 * HALF, :] = (
            t2 * cos_t + t1 * sin_t).astype(BF16)
        qt_ref[0, base + MLA_NOPE + 3 * HALF:base + MLA_QK_PAD, :] = zeros

    for hd in range(MLA_HEADS):
        vt_ref[0, hd * MLA_V_ROWS:hd * MLA_V_ROWS + MLA_V, :] = v_t[hd * MLA_V:(hd + 1) * MLA_V]
        vt_ref[0, hd * MLA_V_ROWS + MLA_V:(hd + 1) * MLA_V_ROWS, :] = ones_row
    kpe = proj[:, C_KPE:C_KS]
    ss_pe = 0.5 * jnp.sum(kpe * kpe, axis=-1, keepdims=True)
    kpe_rot = _rope(kpe * gkr_ref[0], cos, sin)
    for hd in range(MLA_HEADS):
        nope = k_up[:, hd * MLA_NOPE:(hd + 1) * MLA_NOPE]
        ss = jnp.sum(nope * nope, axis=-1, keepdims=True) + ss_pe
        r = lax.rsqrt(ss * (1.0 / MLA_QK) + EPS)
        ka_ref[:, hd * MLA_QK_PAD: hd * MLA_QK_PAD + MLA_NOPE] = (
            nope * r * gkn_ref[0]).astype(BF16)
        ka_ref[:, hd * MLA_QK_PAD + MLA_NOPE:(hd + 1) * MLA_QK_PAD] = (kpe_rot * r).astype(BF16)

    gsq = jnp.tile(gsq_ref[0], (1, tm // LANES))
    for hd in range(SWA_HEADS):
        blk = proj_t[hd * SWA_HEAD_DIM:(hd + 1) * SWA_HEAD_DIM]
        ss = jnp.sum(blk * blk, axis=0, keepdims=True)
        qn = blk * (lax.rsqrt(ss * (1.0 / SWA_HEAD_DIM) + EPS) * swa_scale) * gsq
        t1 = qn[:HALF]
        t2 = qn[HALF:]
        base = hd * LANES
        qbt_ref[0, base:base + HALF, :] = (t1 * cos_t - t2 * sin_t).astype(BF16)
        qbt_ref[0, base + HALF:base + 2 * HALF, :] = zeros
        qbt_ref[0, base + 2 * HALF:base + 3 * HALF, :] = (t2 * cos_t + t1 * sin_t).astype(BF16)
        qbt_ref[0, base + 3 * HALF:base + LANES, :] = zeros
    for c in range(SWA_KV_HEADS):
        vbt_ref[0, c * SWA_V_ROWS:c * SWA_V_ROWS + SWA_HEAD_DIM, :] = (
            proj_t[R_VS + c * SWA_HEAD_DIM:R_VS + (c + 1) * SWA_HEAD_DIM].astype(BF16))
        vbt_ref[0, c * SWA_V_ROWS + SWA_HEAD_DIM:(c + 1) * SWA_V_ROWS, :] = ones_row
    for c in range(SWA_KV_HEADS):
        xk = proj[:, C_KS + c * LANES: C_KS + (c + 1) * LANES]
        ss = 0.5 * jnp.sum(xk * xk, axis=-1, keepdims=True)
        r = lax.rsqrt(ss * (1.0 / SWA_HEAD_DIM) + EPS)
        kb_ref[:, c * LANES:(c + 1) * LANES] = _rope(xk * r * gsk_ref[0], cos, sin).astype(BF16)


N_FFN_CONSTS = 4
N_PROJ_CONSTS = 13
N_PROJ_TABLES = 4


def _ffn_proj_kernel(*refs):
    x_ref = refs[0]
    ffn_refs = refs[1:1 + N_FFN_CONSTS]
    proj_refs = refs[1 + N_FFN_CONSTS:1 + N_FFN_CONSTS + N_PROJ_CONSTS + N_PROJ_TABLES]
    o_ref = refs[1 + N_FFN_CONSTS + N_PROJ_CONSTS + N_PROJ_TABLES]
    proj_out_refs = refs[-7:-1]
    x_prev = refs[-1]

    @pl.when(pl.program_id(0) == 0)
    def _():
        x_prev[...] = jnp.zeros_like(x_prev)

    g_ref, wg_ref, wu_ref, wd_ref = ffn_refs
    x_new = []
    _interleave(_proj_stages(x_prev[...], *proj_refs, *proj_out_refs),
                _ffn_stages(x_ref[...], g_ref[0], wg_ref, wu_ref, wd_ref, x_new))
    o_ref[...] = x_new[0]
    x_prev[...] = x_new[0]


def _ffn_proj_call(x, seq, layer, ffn_consts, consts, tables):
    n = x.shape[0]
    tm = min(TM_ROWS, seq)
    n_tiles = n // tm
    n_seq_tiles = seq // tm
    ffn_tile = lambda i: jnp.minimum(i, n_tiles - 1)
    proj_tile = lambda i: jnp.maximum(i - 1, 0)
    proj_row_spec = lambda width: pl.BlockSpec((tm, width), lambda i: (proj_tile(i), 0))
    ffn_row_spec = pl.BlockSpec((tm, D_MODEL), lambda i: (ffn_tile(i), 0))
    pos_spec = pl.BlockSpec((tm, LANES), lambda i: (proj_tile(i) % n_seq_tiles, 0))
    pos_t_spec = pl.BlockSpec((HALF, tm), lambda i: (0, proj_tile(i) % n_seq_tiles))
    row_widths = {1: MLA_HEADS * MLA_QK_PAD, 4: SWA_KV_HEADS * LANES}
    col_heights = {0: MLA_HEADS * MLA_QK_PAD, 2: MLA_HEADS * MLA_V_ROWS,
                   3: SWA_HEADS * LANES, 5: SWA_KV_HEADS * SWA_V_ROWS}
    out_specs, out_shape = [ffn_row_spec], [jax.ShapeDtypeStruct(x.shape, F32)]
    for idx in range(6):
        if idx in row_widths:
            out_specs.append(proj_row_spec(row_widths[idx]))
            out_shape.append(jax.ShapeDtypeStruct((n, row_widths[idx]), BF16))
        else:
            out_specs.append(pl.BlockSpec((1, col_heights[idx], tm), lambda i: (proj_tile(i), 0, 0)))
            out_shape.append(jax.ShapeDtypeStruct((n_tiles, col_heights[idx], tm), BF16))
    assert len(ffn_consts) == N_FFN_CONSTS and len(consts) == N_PROJ_CONSTS and len(tables) == N_PROJ_TABLES
    return pl.pallas_call(
        _ffn_proj_kernel,
        grid=(n_tiles + 1,),
        in_specs=([ffn_row_spec] + [_layer_spec(c.shape, layer) for c in ffn_consts]
                  + [_layer_spec(c.shape, layer) for c in consts]
                  + [pos_spec, pos_spec, pos_t_spec, pos_t_spec]),
        out_specs=out_specs,
        out_shape=out_shape,
        scratch_shapes=[pltpu.VMEM((tm, D_MODEL), F32)],
        compiler_params=pltpu.CompilerParams(
            dimension_semantics=("arbitrary",), vmem_limit_bytes=VMEM_LIMIT),
        name="ffn_proj",
    )(x, *ffn_consts, *consts, *tables)


MLA_UNROLL = 4
MLA_STREAMS = 4


def _mla_kernel(q_ref, k_tile_ref, v_tile_ref, o_ref, k_ref, v_ref, s0_ref, s1_ref, p0_ref, p1_ref, acc_ref):
    t = q_ref.shape[3]
    tk = t // 2
    qi = pl.program_id(2)
    heads = range(MLA_STREAMS)
    s_refs = (s0_ref, s1_ref)
    p_refs = (p0_ref, p1_ref)

    k_ref[pl.ds(pl.multiple_of(qi * t, t), t), :] = k_tile_ref[0]
    v_ref[qi] = v_tile_ref[0, 0]

    def scores(tile, half, h):
        start = pl.multiple_of(tile * t, t) + half * tk
        k = k_ref[pl.ds(start, tk), h * MLA_QK_PAD:(h + 1) * MLA_QK_PAD]
        return _dot(k, q_ref[0, 0, h * MLA_QK_PAD:(h + 1) * MLA_QK_PAD, :])

    def weighted_values(tile, half, h, buf):
        v = v_ref[tile, h * MLA_V_ROWS:(h + 1) * MLA_V_ROWS, half * tk:(half + 1) * tk]
        return _dot(v, p_refs[buf][h, :, :t])

    def phase(cur, state, nxt_tile, prev_tile, mask_half=None):
        new_state = []
        for h in heads:
            m, alpha_prev, block_max = state[h]
            if prev_tile is not None:
                pv = weighted_values(prev_tile, 1 - cur, h, 1 - cur)
            next_max = block_max
            if nxt_tile is not None:
                nxt = scores(nxt_tile, 1 - cur, h)
                s_refs[1 - cur][h, :, :t] = nxt
                next_max = jnp.max(nxt, axis=0, keepdims=True)
            s = s_refs[cur][h, :, :t]
            if mask_half is not None:
                key = lax.broadcasted_iota(jnp.int32, (tk, t), 0) + mask_half * tk
                qry = lax.broadcasted_iota(jnp.int32, (tk, t), 1)
                s = jnp.where(key <= qry, s, NEG_BIG)
                block_max = jnp.max(s, axis=0, keepdims=True)
            m_new = jnp.maximum(m, block_max)
            p_refs[cur][h, :, :t] = jnp.exp2(s - m_new).astype(BF16)
            alpha = jnp.exp2(m - m_new)
            if prev_tile is not None:
                acc_ref[h, :, :t] = alpha_prev * acc_ref[h, :, :t] + pv
            new_state.append((m_new, alpha, next_max))
        return tuple(new_state)

    def diagonal(state):
        for h in heads:
            acc_ref[h, :, :t] = jnp.zeros((MLA_V_ROWS, t), F32)
            s_refs[0][h, :, :t] = scores(qi, 0, h)
        state = phase(0, state, qi, None, mask_half=0)
        return phase(1, state, 0, qi, mask_half=1)

    def key_tile(jj, state):
        state = phase(0, state, jj, jnp.where(jj == 0, qi, jj - 1))
        return phase(1, state, jj + 1, jj)

    def key_tiles(first, count, state):
        for j in range(count):
            state = key_tile(first + j, state)
        return state

    def finish(state):
        last_tile = jnp.maximum(qi - 1, 0)
        for h in heads:
            alpha_prev = state[h][1]
            acc = alpha_prev * acc_ref[h, :, :t] + weighted_values(last_tile, 1, h, 1)
            o_ref[0, :, h * MLA_V:(h + 1) * MLA_V] = (acc[:MLA_V] / acc[MLA_V:MLA_V + 1]).T
        return 0

    state = tuple((jnp.full((1, t), NEG_BIG, F32), jnp.zeros((1, t), F32), jnp.zeros((1, t), F32))
                  for _ in heads)
    single = qi % 2
    state = lax.cond(single == 1, lambda st: key_tile(0, diagonal(st)), diagonal, state)
    n_groups = (qi - single) // MLA_UNROLL
    state = lax.fori_loop(
        0, n_groups, lambda kk, st: key_tiles(single + MLA_UNROLL * kk, MLA_UNROLL, st), state)
    pair_start = single + MLA_UNROLL * n_groups
    lax.cond(qi - pair_start == 2, lambda st: finish(key_tiles(pair_start, 2, st)), finish, state)


def _mla_call(qt, ka, vt):
    b, nq, _, t = qt.shape
    s = ka.shape[1]
    qw = MLA_STREAMS * MLA_QK_PAD
    vw = MLA_STREAMS * MLA_V_ROWS
    ow = MLA_STREAMS * MLA_V
    return pl.pallas_call(
        _mla_kernel,
        grid=(b, MLA_HEADS // MLA_STREAMS, nq),
        in_specs=[pl.BlockSpec((1, 1, qw, t), lambda bi, h, i: (bi, i, h, 0)),
                  pl.BlockSpec((1, t, qw), lambda bi, h, i: (bi, i, h)),
                  pl.BlockSpec((1, 1, vw, t), lambda bi, h, i: (bi, i, h, 0))],
        out_specs=pl.BlockSpec((1, t, ow), lambda bi, h, i: (bi, i, h)),
        out_shape=jax.ShapeDtypeStruct((b, s, MLA_WIDTH), F32),
        scratch_shapes=[pltpu.VMEM((s, qw), BF16),
                        pltpu.VMEM((nq, vw, t), BF16),
                        pltpu.VMEM((MLA_STREAMS, t // 2, t + LANES), F32),
                        pltpu.VMEM((MLA_STREAMS, t // 2, t + LANES), F32),
                        pltpu.VMEM((MLA_STREAMS, t // 2, t + LANES), BF16),
                        pltpu.VMEM((MLA_STREAMS, t // 2, t + LANES), BF16),
                        pltpu.VMEM((MLA_STREAMS, MLA_V_ROWS, t + LANES), F32)],
        compiler_params=pltpu.CompilerParams(
            dimension_semantics=("arbitrary", "arbitrary", "arbitrary"),
            vmem_limit_bytes=VMEM_LIMIT),
        name="mla_attn",
    )(qt, ka, vt)


def _swa_kernel(sink_ref, q_ref, kc_ref, kp_ref, vc_ref, vp_ref, o_ref, kf_ref):
    blk = SWA_BLOCK
    tiles, t = q_ref.shape[1], q_ref.shape[3]
    per_tile = t // blk
    n_blk = tiles * per_tile
    tile = pl.program_id(1)
    kf_ref[0:blk, :] = kp_ref[0]
    kf_ref[blk:, :] = kc_ref[0]

    cols = SWA_GROUP * blk
    k_rel = lax.broadcasted_iota(jnp.int32, (2 * blk, cols), 0)
    q_rel = lax.broadcasted_iota(jnp.int32, (2 * blk, cols), 1) & (blk - 1)
    in_window = (k_rel > q_rel) & (k_rel <= q_rel + blk)
    head_of_col = lax.broadcasted_iota(jnp.int32, (1, cols), 1) // blk

    sinks = []
    for c in range(SWA_KV_HEADS):
        sink = jnp.zeros((1, cols), F32)
        for g in range(SWA_GROUP):
            sink = jnp.where(head_of_col == g, sink_ref[c * SWA_GROUP + g] * LOG2E, sink)
        sinks.append(sink)

    def band_scores(c, n):
        q_t = jnp.concatenate(
            [q_ref[0, n // per_tile, (c * SWA_GROUP + g) * LANES:(c * SWA_GROUP + g + 1) * LANES,
                   (n % per_tile) * blk:(n % per_tile + 1) * blk]
             for g in range(SWA_GROUP)], axis=1)
        k = kf_ref[n * blk:(n + 2) * blk, c * LANES:(c + 1) * LANES]
        return _dot(k, q_t)

    def band_softmax(c, n, s):
        valid = in_window
        if n == 0:
            valid = valid & ((k_rel >= blk) | (tile > 0))
        s = jnp.where(valid, s, NEG_BIG)
        m = jnp.maximum(jnp.max(s, axis=0, keepdims=True), sinks[c])
        return jnp.exp2(s - m).astype(BF16), m

    def band_output(c, n, p, m):
        v_rows = slice(c * SWA_V_ROWS, (c + 1) * SWA_V_ROWS)
        a, local = n // per_tile, n % per_tile
        if n == 0:
            v_t = jnp.concatenate([vp_ref[0, 0, v_rows, :], vc_ref[0, 0, v_rows, 0:blk]], axis=1)
        elif local == 0:
            v_t = jnp.concatenate([vc_ref[0, a - 1, v_rows, t - blk:t], vc_ref[0, a, v_rows, 0:blk]], axis=1)
        else:
            v_t = vc_ref[0, a, v_rows, (local - 1) * blk:(local + 1) * blk]
        o_t = _dot(v_t, p)
        denom = o_t[SWA_HEAD_DIM:SWA_HEAD_DIM + 1] + jnp.exp2(sinks[c] - m)
        o = o_t[:SWA_HEAD_DIM] / denom
        for j in range(SWA_GROUP // 2):
            pair_t = jnp.concatenate([o[:, (2 * j) * blk:(2 * j + 1) * blk],
                                      o[:, (2 * j + 1) * blk:(2 * j + 2) * blk]], axis=0)
            col = (c * SWA_GROUP // 2 + j) * LANES
            o_ref[0, n * blk:(n + 1) * blk, col:col + LANES] = pair_t.T

    bands = [(c, n) for c in range(SWA_KV_HEADS) for n in range(n_blk)]
    score_lead, softmax_lead = 3, 1
    s_vals, p_vals = {}, {}
    for step in range(len(bands) + score_lead):
        done = step - score_lead
        if done >= 0:
            band_output(*bands[done], *p_vals.pop(done))
        if step < len(bands):
            s_vals[step] = band_scores(*bands[step])
        ready = step - (score_lead - softmax_lead)
        if 0 <= ready < len(bands):
            p_vals[ready] = band_softmax(*bands[ready], s_vals.pop(ready))


def _swa_call(qbt, kb, vbt, sinks):
    b, n_tiles, qh, t = qbt.shape
    s = kb.shape[1]
    ratio = t // SWA_BLOCK
    tiles = min(SWA_TILES_PER_STEP, n_tiles)
    rows = tiles * t
    kw = SWA_KV_HEADS * LANES
    vh = SWA_KV_HEADS * SWA_V_ROWS
    return pl.pallas_call(
        _swa_kernel,
        grid=(b, n_tiles // tiles),
        in_specs=[pl.BlockSpec(memory_space=pltpu.SMEM),
                  pl.BlockSpec((1, tiles, qh, t), lambda bi, i: (bi, i, 0, 0)),
                  pl.BlockSpec((1, rows, kw), lambda bi, i: (bi, i, 0)),
                  pl.BlockSpec((1, SWA_BLOCK, kw),
                               lambda bi, i: (bi, jnp.maximum(i * tiles * ratio - 1, 0), 0)),
                  pl.BlockSpec((1, tiles, vh, t), lambda bi, i: (bi, i, 0, 0)),
                  pl.BlockSpec((1, 1, vh, SWA_BLOCK),
                               lambda bi, i: (bi, jnp.maximum(i * tiles - 1, 0), 0, ratio - 1))],
        out_specs=pl.BlockSpec((1, rows, SWA_WIDTH), lambda bi, i: (bi, i, 0)),
        out_shape=jax.ShapeDtypeStruct((b, s, SWA_WIDTH), F32),
        scratch_shapes=[pltpu.VMEM((rows + SWA_BLOCK, kw), BF16)],
        compiler_params=pltpu.CompilerParams(
            dimension_semantics=("arbitrary", "arbitrary"), vmem_limit_bytes=VMEM_LIMIT),
        name="swa_attn",
    )(sinks, qbt, kb, kb, vbt, vbt)


def _out_ffn_kernel(x_ref, oa_ref, ob_ref, ga_ref, gb_ref, wo_ref, g_ref, wg_ref, wu_ref, wd_ref, o_ref):
    na = _rms(oa_ref[...], ga_ref[0], MLA_WIDTH).astype(BF16)
    nb = _rms(ob_ref[...], gb_ref[0], SWA_WIDTH).astype(BF16)
    y = _dot(na, wo_ref[0, 0:MLA_WIDTH, :]) + _dot(nb, wo_ref[0, MLA_WIDTH:, :])
    x_new = []
    _interleave(_ffn_stages(x_ref[...] + y, g_ref[0], wg_ref, wu_ref, wd_ref, x_new))
    o_ref[...] = x_new[0]


def _out_ffn_call(x, oa, ob, layer, ga, gb, wo, ffn_consts):
    n = x.shape[0]
    tm = min(TM_ROWS, n)
    row_spec = lambda width: pl.BlockSpec((tm, width), lambda i: (i, 0))
    return pl.pallas_call(
        _out_ffn_kernel,
        grid=(n // tm,),
        in_specs=[row_spec(D_MODEL), row_spec(MLA_WIDTH), row_spec(SWA_WIDTH),
                  _layer_spec(ga.shape, layer), _layer_spec(gb.shape, layer), _layer_spec(wo.shape, layer)]
                 + [_layer_spec(c.shape, layer) for c in ffn_consts],
        out_specs=row_spec(D_MODEL),
        out_shape=jax.ShapeDtypeStruct(x.shape, F32),
        compiler_params=pltpu.CompilerParams(
            dimension_semantics=("arbitrary",), vmem_limit_bytes=VMEM_LIMIT),
        name="out_ffn",
    )(x, oa, ob, ga, gb, wo, *ffn_consts)


def _take_cols(w, cols):
    cols = np.asarray(cols)
    cuts = [0] + [i for i in range(1, len(cols)) if cols[i] != cols[i - 1] + 1] + [len(cols)]
    runs = [w[..., int(cols[a]):int(cols[b - 1]) + 1] for a, b in zip(cuts[:-1], cuts[1:])]
    return runs[0] if len(runs) == 1 else jnp.concatenate(runs, axis=-1)


def _dup_cols(n_heads, head_dim, base=0):
    half = head_dim // 2
    cols = []
    for hd in range(n_heads):
        for part in range(2):
            start = base + hd * head_dim + part * half
            cols.extend(list(range(start, start + half)) * 2)
    return np.asarray(cols, np.int32)


def _proj_consts(mix_norm, w_in, mla_q_a_norm, mla_w_q_b, mla_kv_a_norm, mla_w_kv_b,
                 mla_q_norm, mla_k_norm, swa_q_norm, swa_k_norm):
    o_kpe = MLA_Q_RANK + MLA_KV_RANK
    o_qs = o_kpe + MLA_ROPE
    o_ks = o_qs + SWA_WIDTH
    o_vs = o_ks + SWA_KV_HEADS * SWA_HEAD_DIM
    in_cols = np.concatenate([
        np.arange(0, o_kpe, dtype=np.int32),
        _dup_cols(1, MLA_ROPE, o_kpe),
        _dup_cols(SWA_KV_HEADS, SWA_HEAD_DIM, o_ks)])
    ft_cols = np.concatenate([
        np.arange(o_qs, o_ks, dtype=np.int32),
        np.arange(o_vs, o_vs + SWA_KV_HEADS * SWA_HEAD_DIM, dtype=np.int32)])
    k_cols = np.concatenate(
        [np.arange(h * (MLA_NOPE + MLA_V), h * (MLA_NOPE + MLA_V) + MLA_NOPE) for h in range(MLA_HEADS)])
    v_cols = np.concatenate(
        [np.arange(h * (MLA_NOPE + MLA_V) + MLA_NOPE, (h + 1) * (MLA_NOPE + MLA_V)) for h in range(MLA_HEADS)])
    rope_gain = _dup_cols(1, MLA_ROPE, MLA_NOPE)
    swa_gain = _dup_cols(1, SWA_HEAD_DIM)
    t = lambda w: jnp.swapaxes(w, 1, 2)
    lane_rep = lambda v: jnp.broadcast_to(v.astype(F32)[:, :, None], v.shape + (LANES,))
    return (
        _rows(mix_norm),
        _take_cols(w_in, in_cols).astype(BF16),
        t(_take_cols(w_in, ft_cols)).astype(BF16),
        _rows(mla_q_a_norm),
        t(mla_w_q_b).astype(BF16),
        _rows(mla_kv_a_norm),
        _take_cols(mla_w_kv_b, k_cols).astype(BF16),
        t(_take_cols(mla_w_kv_b, v_cols)).astype(BF16),
        lane_rep(mla_q_norm),
        _rows(mla_k_norm[:, :MLA_NOPE]),
        _rows(_take_cols(mla_k_norm, rope_gain)),
        lane_rep(swa_q_norm),
        _rows(_take_cols(swa_k_norm, swa_gain)),
    )


def _rows(v):
    return v[:, None, :].astype(F32)


def _rope_tables(seq):
    pos = jnp.arange(seq, dtype=F32)
    inv = 1.0 / (ROPE_THETA ** (jnp.arange(0, 2 * HALF, 2, dtype=F32) / (2 * HALF)))
    ang = pos[:, None] * inv[None, :]
    c, s = jnp.cos(ang), jnp.sin(ang)
    return (jnp.concatenate([c, c, c, c], axis=1), jnp.concatenate([-s, -s, s, s], axis=1), c.T, s.T)


def kernel(x, ffn1_norm, ffn1_w_gate, ffn1_w_up, ffn1_w_down, mix_norm, w_in, mla_q_a_norm, mla_w_q_b, mla_kv_a_norm, mla_w_kv_b, mla_q_norm, mla_k_norm, swa_q_norm, swa_k_norm, swa_sinks, mla_out_norm, swa_out_norm, w_o, ffn2_norm, ffn2_w_gate, ffn2_w_up, ffn2_w_down):
    b, s, d = x.shape
    depth = w_in.shape[0]
    tables = _rope_tables(s)
    ffn1 = (_rows(ffn1_norm), ffn1_w_gate.astype(BF16), ffn1_w_up.astype(BF16), ffn1_w_down.astype(BF16))
    ffn2 = (_rows(ffn2_norm), ffn2_w_gate.astype(BF16), ffn2_w_up.astype(BF16), ffn2_w_down.astype(BF16))
    consts = _proj_consts(mix_norm, w_in, mla_q_a_norm, mla_w_q_b, mla_kv_a_norm, mla_w_kv_b,
                          mla_q_norm, mla_k_norm, swa_q_norm, swa_k_norm)
    out_consts = (_rows(mla_out_norm), _rows(swa_out_norm), w_o.astype(BF16))
    xf = x.reshape(b * s, d)
    for l in range(depth):
        xf, qt, ka, vt, qbt, kb, vbt = _ffn_proj_call(xf, s, l, ffn1, consts, tables)
        shp = lambda t: t.reshape(b, s, t.shape[-1])
        tiles = lambda t: t.reshape(b, -1, t.shape[-2], t.shape[-1])
        oa = _mla_call(tiles(qt), shp(ka), tiles(vt))
        ob = _swa_call(tiles(qbt), shp(kb), tiles(vbt), swa_sinks[l].astype(F32))
        xf = _out_ffn_call(xf, oa.reshape(b * s, MLA_WIDTH), ob.reshape(b * s, SWA_WIDTH), l,
                           *out_consts, ffn2)
    return xf.reshape(b, s, d)
```
